```python
import math
import jax, jax.numpy as jnp
from jax import lax
import numpy as np

D_MODEL = 1024
BATCH = 8
SEQ = 8192
DEPTH = 2
DEC_BATCH = 4
DEC_SEQ = 4096
PAST_LEN = 128

GM_WIDTH = 1024
GM_GROUPS = 8
CHUNK = 128
N_HEADS = 8
HEAD_DIM = 64
ATT_WIDTH = N_HEADS * 2 * HEAD_DIM
Q_BLOCK = 128
SPLIT_AT = np.cumsum([GM_WIDTH, GM_WIDTH, ATT_WIDTH, ATT_WIDTH, ATT_WIDTH, D_MODEL]).tolist()
IN_COLS = 2 * GM_WIDTH + 3 * ATT_WIDTH + 2 * D_MODEL
N_EXPERTS = 16
N_GROUPS = 4
EXPERTS_PER_GROUP = N_EXPERTS // N_GROUPS
TOP_K = 2
D_EXPERT = 512
N_MOD = 6
EPS = 1e-6

kernel_name = 'hybrid_gmlp_diffattn_moe_encoder'


def rmsnorm(x, g):
    xf = x.astype(jnp.float32)
    y = xf * lax.rsqrt(jnp.mean(xf * xf, axis=-1, keepdims=True) + EPS)
    return (y * g.astype(jnp.float32)).astype(x.dtype)


def lambda_init(layer):
    return 0.8 - 0.6 * math.exp(-0.3 * layer)


def alibi_slopes():
    return jnp.exp2(-(8.0 / N_HEADS) * jnp.arange(1, N_HEADS + 1, dtype=jnp.float32))


def spatial_gating(u, v, g_sg, w_s, b_s):
    B, S, _ = v.shape
    v = rmsnorm(v, g_sg)
    vc = v.reshape(B, S // CHUNK, CHUNK, GM_GROUPS, GM_WIDTH // GM_GROUPS)
    mixed = jnp.einsum('gts,bnsgc->bntgc', w_s, vc) + b_s.T[None, None, :, :, None]
    return u * mixed.reshape(B, S, GM_WIDTH)


def diff_attention(q, k, v, lam, slopes):
    B, S = q.shape[:2]
    nblk = S // Q_BLOCK
    scale = HEAD_DIM ** -0.5
    kpos = jnp.arange(S, dtype=jnp.float32)
    qb = q.reshape(B, nblk, Q_BLOCK, N_HEADS, 2, HEAD_DIM).transpose(1, 0, 2, 3, 4, 5)

    def block(args):
        qi, i = args
        s = jnp.einsum('bqhmd,bkhmd->bhmqk', qi, k).astype(jnp.float32) * scale
        qpos = (i * Q_BLOCK + jnp.arange(Q_BLOCK)).astype(jnp.float32)
        dist = jnp.abs(qpos[:, None] - kpos[None, :])
        s = s - slopes[None, :, None, None, None] * dist[None, None, None]
        a = jax.nn.softmax(s, axis=-1)
        w = a[:, :, 0] - lam * a[:, :, 1]
        return jnp.einsum('bhqk,bkhe->bqhe', w.astype(v.dtype), v)

    out = lax.map(block, (qb, jnp.arange(nblk)))
    return out.transpose(1, 0, 2, 3, 4).reshape(B, S, N_HEADS, 2 * HEAD_DIM)


def moe(h, w_router, b_router, w_gate, w_up, w_down):
    B, S, D = h.shape
    t = h.reshape(B * S, D)
    scores = jax.nn.sigmoid(jnp.dot(t, w_router).astype(jnp.float32))
    biased = scores + b_router.astype(jnp.float32)
    grp = biased.reshape(-1, N_GROUPS, EXPERTS_PER_GROUP)
    grp_score = lax.top_k(grp, TOP_K)[0].sum(-1)
    sel_grp = jnp.argmax(grp_score, axis=-1)
    in_grp = (jnp.arange(N_EXPERTS) // EXPERTS_PER_GROUP)[None, :] == sel_grp[:, None]
    masked = jnp.where(in_grp, biased, -jnp.inf)
    _, idx = lax.top_k(masked, TOP_K)
    wts = jnp.take_along_axis(scores, idx, axis=-1)
    wts = wts / jnp.sum(wts, axis=-1, keepdims=True)
    comb = jnp.sum(jax.nn.one_hot(idx, N_EXPERTS, dtype=jnp.float32) * wts[..., None], axis=1).astype(h.dtype)
    y = jnp.zeros_like(t)
    for e in range(N_EXPERTS):
        hid = jax.nn.silu(jnp.dot(t, w_gate[e])) * jnp.dot(t, w_up[e])
        y = y + comb[:, e:e + 1] * jnp.dot(hid, w_down[e])
    return y.reshape(B, S, D)


def trunk(x, c, w_ada, b_ada, g_mix, w_in, g_sg, w_s, b_s, w_br_a, lam_q1, lam_k1, lam_q2, lam_k2,
          g_subln, w_br_b, w_out, g_ffn, w_router, b_router, w_gate, w_up, w_down, g_final):
    B, S, _ = x.shape
    slopes = alibi_slopes()
    for l in range(DEPTH):
        mod = jnp.dot(jax.nn.silu(c), w_ada[l]) + b_ada[l]
        sh_m, sc_m, gt_m, sh_f, sc_f, gt_f = jnp.split(mod[:, None, :], N_MOD, axis=-1)
        h = rmsnorm(x, g_mix[l]) * (1 + sc_m) + sh_m
        z = jnp.dot(h, w_in[l])
        u, v, q, k, va, ga, gb = jnp.split(z, SPLIT_AT, axis=-1)
        ya = jnp.dot(spatial_gating(jax.nn.gelu(u), jax.nn.gelu(v), g_sg[l], w_s[l], b_s[l]), w_br_a[l])
        lam_i = lambda_init(l)
        lam = (jnp.exp(jnp.sum(lam_q1[l].astype(jnp.float32) * lam_k1[l].astype(jnp.float32)))
               - jnp.exp(jnp.sum(lam_q2[l].astype(jnp.float32) * lam_k2[l].astype(jnp.float32))) + lam_i)
        o = diff_attention(q.reshape(B, S, N_HEADS, 2, HEAD_DIM), k.reshape(B, S, N_HEADS, 2, HEAD_DIM),
                           va.reshape(B, S, N_HEADS, 2 * HEAD_DIM), lam, slopes)
        o = rmsnorm(o, g_subln[l]) * (1 - lam_i)
        yb = jnp.dot(o.reshape(B, S, ATT_WIDTH), w_br_b[l])
        merged = jax.nn.sigmoid(ga) * ya + jax.nn.sigmoid(gb) * yb
        x = x + gt_m * jnp.dot(merged, w_out[l])
        h = rmsnorm(x, g_ffn[l]) * (1 + sc_f) + sh_f
        x = x + gt_f * moe(h, w_router, b_router, w_gate[l], w_up[l], w_down[l])
    return rmsnorm(x, g_final)


def setup_inputs(seed: int = 0) -> dict:
    key = jax.random.key(seed)
    ks = jax.random.split(key, 32)
    D = D_MODEL

    def nrm(k, shape, s):
        return jax.random.normal(k, shape, jnp.float32) * s

    return {
        'x_prompt': nrm(ks[0], (BATCH, SEQ, D), 1.0),
        'x_sample': nrm(ks[1], (DEC_BATCH, DEC_SEQ, D), 1.0),
        'c_prompt': nrm(ks[2], (BATCH, D), 1.0),
        'c_sample': nrm(ks[3], (DEC_BATCH, D), 1.0),
        'w_ada': nrm(ks[4], (DEPTH, D, N_MOD * D), 0.5 * D ** -0.5),
        'b_ada': nrm(ks[5], (DEPTH, N_MOD * D), 0.02),
        'g_mix': 1.0 + nrm(ks[6], (DEPTH, D), 0.02),
        'w_in': nrm(ks[7], (DEPTH, D, IN_COLS), D ** -0.5),
        'g_sg': 1.0 + nrm(ks[8], (DEPTH, GM_WIDTH), 0.02),
        'w_s': nrm(ks[9], (DEPTH, GM_GROUPS, CHUNK, CHUNK), CHUNK ** -0.5),
        'b_s': 1.0 + nrm(ks[10], (DEPTH, GM_GROUPS, CHUNK), 0.1),
        'w_br_a': nrm(ks[11], (DEPTH, GM_WIDTH, D), GM_WIDTH ** -0.5),
        'lam_q1': nrm(ks[12], (DEPTH, HEAD_DIM), 0.1),
        'lam_k1': nrm(ks[13], (DEPTH, HEAD_DIM), 0.1),
        'lam_q2': nrm(ks[14], (DEPTH, HEAD_DIM), 0.1),
        'lam_k2': nrm(ks[15], (DEPTH, HEAD_DIM), 0.1),
        'g_subln': 1.0 + nrm(ks[16], (DEPTH, 2 * HEAD_DIM), 0.02),
        'w_br_b': nrm(ks[17], (DEPTH, ATT_WIDTH, D), ATT_WIDTH ** -0.5),
        'w_out': nrm(ks[18], (DEPTH, D, D), D ** -0.5),
        'g_ffn': 1.0 + nrm(ks[19], (DEPTH, D), 0.02),
        'w_router': nrm(ks[20], (D, N_EXPERTS), D ** -0.5),
        'b_router': nrm(ks[21], (N_EXPERTS,), 0.01),
        'w_gate': nrm(ks[22], (DEPTH, N_EXPERTS, D, D_EXPERT), D ** -0.5),
        'w_up': nrm(ks[23], (DEPTH, N_EXPERTS, D, D_EXPERT), D ** -0.5),
        'w_down': nrm(ks[24], (DEPTH, N_EXPERTS, D_EXPERT, D), D_EXPERT ** -0.5),
        'g_final': 1.0 + nrm(ks[25], (D,), 0.02),
    }


def reference(x_prompt, x_sample, c_prompt, c_sample, w_ada, b_ada, g_mix, w_in, g_sg, w_s, b_s, w_br_a,
              lam_q1, lam_k1, lam_q2, lam_k2, g_subln, w_br_b, w_out, g_ffn, w_router, b_router,
              w_gate, w_up, w_down, g_final):
    weights = (w_ada, b_ada, g_mix, w_in, g_sg, w_s, b_s, w_br_a, lam_q1, lam_k1, lam_q2, lam_k2,
               g_subln, w_br_b, w_out, g_ffn, w_router, b_router, w_gate, w_up, w_down, g_final)
    y_prompt = trunk(x_prompt, c_prompt, *weights)
    y_sample = trunk(x_sample, c_sample, *weights)
    return (y_prompt, y_sample)
```

```python
import functools
import math

import jax
import jax.numpy as jnp
from jax import lax
from jax.experimental import pallas as pl
from jax.experimental.pallas import tpu as pltpu

D_MODEL = 1024
N_HEADS = 8
HEAD_DIM = 64
HEAD_WIDTH = 2 * HEAD_DIM
CHUNK = 128
GM_GROUPS = 8
GROUP_WIDTH = D_MODEL // GM_GROUPS
N_EXPERTS = 16
N_GROUPS = 4
EXPERTS_PER_GROUP = N_EXPERTS // N_GROUPS
D_EXPERT = 512
N_MOD = 6
EPS = 1e-6
COL_U, COL_V, COL_Q, COL_K, COL_VA, COL_GA, COL_GB = range(7)
N_COL_BLOCKS = 7
LANES = 128
NEG_BIG = -1e30

F32 = jnp.float32
BF16 = jnp.bfloat16

VMEM_LIMIT = 56 * 1024 * 1024


def _params(n_axes):
    return pltpu.CompilerParams(dimension_semantics=("arbitrary",) * n_axes,
                                vmem_limit_bytes=VMEM_LIMIT)


def _lambda_init(layer):
    return 0.8 - 0.6 * math.exp(-0.3 * layer)


def _dot(a, b):
    return jnp.dot(a, b, preferred_element_type=F32)


def _mod_kernel(c_ref, w_ref, b_ref, o_ref):
    c = c_ref[...]
    a = c * jax.nn.sigmoid(c)
    o_ref[...] = _dot(a.astype(BF16), w_ref[...].astype(BF16)) + b_ref[...]


def _modulations(c_all, w_ada, b_ada):
    depth, d, n = w_ada.shape
    bp = c_all.shape[0]
    tn = n // 4
    return pl.pallas_call(
        _mod_kernel,
        out_shape=jax.ShapeDtypeStruct((depth, bp, n), F32),
        grid=(depth, n // tn),
        in_specs=[
            pl.BlockSpec((bp, d), lambda l, j: (0, 0)),
            pl.BlockSpec((None, d, tn), lambda l, j: (l, 0, j)),
            pl.BlockSpec((None, 1, tn), lambda l, j: (l, 0, j)),
        ],
        out_specs=pl.BlockSpec((None, bp, tn), lambda l, j: (l, 0, j)),
        compiler_params=_params(2),
        name="adaln_mod",
    )(c_all, w_ada, b_ada.reshape(depth, 1, n))


def _rms(x):
    return x * lax.rsqrt(jnp.mean(x * x, axis=-1, keepdims=True) + EPS)


def _in_kernel(x_ref, sh_ref, sc_ref, g_ref, w_ref, gsg_ref, z_ref, h_ref, *, rows):
    j = pl.program_id(1)
    tm = x_ref.shape[0]
    n_chunks = tm // rows

    @pl.when(j == 0)
    def _():
        for r in range(n_chunks):
            sl = pl.ds(r * rows, rows)
            h = _rms(x_ref[sl, :]) * g_ref[...]
            h = h * (1.0 + sc_ref[...]) + sh_ref[...]
            h_ref[sl, :] = h.astype(BF16)

    def project(epilogue):
        for r in range(n_chunks):
            sl = pl.ds(r * rows, rows)
            z_ref[sl, :] = epilogue(_dot(h_ref[sl, :], w_ref[...])).astype(BF16)

    @pl.when(j == COL_U)
    def _():
        project(jax.nn.gelu)

    @pl.when(j == COL_V)
    def _():
        project(lambda t: _rms(jax.nn.gelu(t)) * gsg_ref[...])

    @pl.when(j == COL_Q)
    def _():
        project(lambda t: t * (HEAD_DIM ** -0.5))

    @pl.when((j == COL_K) | (j == COL_VA))
    def _():
        project(lambda t: t)

    @pl.when((j == COL_GA) | (j == COL_GB))
    def _():
        project(jax.nn.sigmoid)


def _in_projection(x, mod3, mod_base, seq, layer, g_mix3, w_in, g_sg3, *, tm):
    t, d = x.shape
    tiles_per_seq = seq // tm

    def mod_row(which):
        return lambda i, j: (mod_base + (i // tiles_per_seq) * N_MOD + which, 0, 0)

    return pl.pallas_call(
        functools.partial(_in_kernel, rows=min(tm, 256)),
        out_shape=jax.ShapeDtypeStruct((t, N_COL_BLOCKS * d), BF16),
        grid=(t // tm, N_COL_BLOCKS),
        in_specs=[
            pl.BlockSpec((tm, d), lambda i, j: (i, 0)),
            pl.BlockSpec((None, 1, d), mod_row(0)),
            pl.BlockSpec((None, 1, d), mod_row(1)),
            pl.BlockSpec((None, 1, d), lambda i, j: (layer, 0, 0)),
            pl.BlockSpec((None, d, d), lambda i, j: (layer, 0, j)),
            pl.BlockSpec((None, 1, d), lambda i, j: (layer, 0, 0)),
        ],
        out_specs=pl.BlockSpec((tm, d), lambda i, j: (i, j)),
        scratch_shapes=[pltpu.VMEM((tm, d), BF16)],
        compiler_params=_params(2),
        name="in_projection",
    )(x, mod3, mod3, g_mix3, w_in, g_sg3)


def _attn_kernel(q_ref, k_ref, v_ref, slope_ref, lq1_ref, lk1_ref, lq2_ref, lk2_ref, gsub_ref,
                 o_ref, vt_ref, acc0_ref, acc1_ref, *, tk, lam_init):
    i = pl.program_id(2)
    tq = q_ref.shape[0]
    n_k = k_ref.shape[0] // tk

    @pl.when(i == 0)
    def _():
        for j in range(n_k):
            vt_ref[j] = v_ref[pl.ds(j * tk, tk), :].T

    q = q_ref[...]
    lane = lax.broadcasted_iota(jnp.int32, q.shape, 1)
    zero = jnp.zeros_like(q)
    qt0 = jnp.where(lane < HEAD_DIM, q, zero).T
    qt1 = jnp.where(lane >= HEAD_DIM, q, zero).T
    slope = slope_ref[...]
    rel = (lax.broadcasted_iota(jnp.int32, (tk, tq), 0)
           - lax.broadcasted_iota(jnp.int32, (tk, tq), 1)).astype(F32)
    q_start = i * tq

    acc0_ref[...] = jnp.zeros_like(acc0_ref)
    acc1_ref[...] = jnp.zeros_like(acc1_ref)

    def update(s, m, l, acc_ref, vt):
        m_new = jnp.maximum(m, jnp.max(s, axis=0, keepdims=True))
        alpha = jnp.exp(m - m_new)
        p = jnp.exp(s - m_new)
        l_new = alpha * l + jnp.sum(p, axis=0, keepdims=True)
        acc_ref[...] = alpha * acc_ref[...] + _dot(vt, p.astype(BF16))
        return m_new, l_new

    def body(j, carry):
        m0, l0, m1, l1 = carry
        kt = k_ref[pl.ds(pl.multiple_of(j * tk, tk), tk), :]
        bias = slope * jnp.abs(rel + (j * tk - q_start).astype(F32))
        vt = vt_ref[j]
        m0, l0 = update(_dot(kt, qt0) - bias, m0, l0, acc0_ref, vt)
        m1, l1 = update(_dot(kt, qt1) - bias, m1, l1, acc1_ref, vt)
        return m0, l0, m1, l1

    init_m = jnp.full((1, tq), NEG_BIG, F32)
    init_l = jnp.zeros((1, tq), F32)
    _, l0, _, l1 = lax.fori_loop(0, n_k, body, (init_m, init_l, init_m, init_l))

    lam = (jnp.exp(jnp.sum(lq1_ref[...] * lk1_ref[...], axis=-1, keepdims=True))
           - jnp.exp(jnp.sum(lq2_ref[...] * lk2_ref[...], axis=-1, keepdims=True)) + lam_init)
    out_t = acc0_ref[...] / l0 - lam * (acc1_ref[...] / l1)
    out = _rms(out_t.T) * gsub_ref[...] * (1.0 - lam_init)
    o_ref[...] = out.astype(BF16)


def _attention(z, batch, seq, layer, slopes3, lam_params, g_subln3, *, tq, tk):
    t = z.shape[0]
    n_q = seq // tq
    blocks_per_col = D_MODEL // HEAD_WIDTH
    lam_spec = pl.BlockSpec((None, 1, HEAD_DIM), lambda b, h, i: (layer, 0, 0))
    return pl.pallas_call(
        functools.partial(_attn_kernel, tk=tk, lam_init=_lambda_init(layer)),
        out_shape=jax.ShapeDtypeStruct((t, D_MODEL), BF16),
        grid=(batch, N_HEADS, n_q),
        in_specs=[
            pl.BlockSpec((tq, HEAD_WIDTH), lambda b, h, i: (b * n_q + i, COL_Q * blocks_per_col + h)),
            pl.BlockSpec((seq, HEAD_WIDTH), lambda b, h, i: (b, COL_K * blocks_per_col + h)),
            pl.BlockSpec((seq, HEAD_WIDTH), lambda b, h, i: (b, COL_VA * blocks_per_col + h)),
            pl.BlockSpec((None, 1, 1), lambda b, h, i: (h, 0, 0)),
            lam_spec, lam_spec, lam_spec, lam_spec,
            pl.BlockSpec((None, 1, HEAD_WIDTH), lambda b, h, i: (layer, 0, 0)),
        ],
        out_specs=pl.BlockSpec((tq, HEAD_WIDTH), lambda b, h, i: (b * n_q + i, h)),
        scratch_shapes=[
            pltpu.VMEM((seq // tk, HEAD_WIDTH, tk), BF16),
            pltpu.VMEM((HEAD_WIDTH, tq), F32),
            pltpu.VMEM((HEAD_WIDTH, tq), F32),
        ],
        compiler_params=_params(3),
        name="diff_attention",
    )(z, z, z, slopes3, *lam_params, g_subln3)


def _route(scores_t, biased_t):
    def row(a, e):
        return a[e:e + 1, :]

    grp_scores = []
    for g in range(N_GROUPS):
        a, b, c, d = (row(biased_t, g * EXPERTS_PER_GROUP + k) for k in range(EXPERTS_PER_GROUP))
        hi_ab, lo_ab = jnp.maximum(a, b), jnp.minimum(a, b)
        hi_cd, lo_cd = jnp.maximum(c, d), jnp.minimum(c, d)
        top1 = jnp.maximum(hi_ab, hi_cd)
        top2 = jnp.maximum(jnp.minimum(hi_ab, hi_cd), jnp.maximum(lo_ab, lo_cd))
        grp_scores.append(top1 + top2)
    best = grp_scores[0]
    sel = jnp.zeros_like(best, dtype=jnp.int32)
    for g in range(1, N_GROUPS):
        better = grp_scores[g] > best
        best = jnp.where(better, grp_scores[g], best)
        sel = jnp.where(better, g, sel)

    def pick(a, k):
        out = row(a, k)
        for g in range(1, N_GROUPS):
            out = jnp.where(sel == g, row(a, g * EXPERTS_PER_GROUP + k), out)
        return out

    cand_b = [pick(biased_t, k) for k in range(EXPERTS_PER_GROUP)]
    cand_s = [pick(scores_t, k) for k in range(EXPERTS_PER_GROUP)]
    v0, k0, w0 = cand_b[0], jnp.zeros_like(sel), cand_s[0]
    for k in range(1, EXPERTS_PER_GROUP):
        better = cand_b[k] > v0
        v0 = jnp.where(better, cand_b[k], v0)
        k0 = jnp.where(better, k, k0)
        w0 = jnp.where(better, cand_s[k], w0)
    v1 = jnp.full_like(v0, -jnp.inf)
    k1 = jnp.zeros_like(sel)
    w1 = jnp.zeros_like(w0)
    for k in range(EXPERTS_PER_GROUP):
        better = (k0 != k) & (cand_b[k] > v1)
        v1 = jnp.where(better, cand_b[k], v1)
        k1 = jnp.where(better, k, k1)
        w1 = jnp.where(better, cand_s[k], w1)
    total = w0 + w1
    e0 = sel * EXPERTS_PER_GROUP + k0
    e1 = sel * EXPERTS_PER_GROUP + k1
    return e0, e1, w0 / total, w1 / total


def _post_kernel(u_ref, v_ref, ga_ref, gb_ref, o_ref, x_ref, gt_ref, shf_ref, scf_ref, gffn_ref,
                 ws_ref, bs_ref, wa_ref, wb_ref, wo_ref, wr_ref, br_ref,
                 xmid_ref, h2_ref, comb_ref, a_ref):
    tm = x_ref.shape[0]
    for n in range(tm // CHUNK):
        rows = pl.ds(n * CHUNK, CHUNK)
        for g in range(GM_GROUPS):
            cols = pl.ds(g * GROUP_WIDTH, GROUP_WIDTH)
            mixed = _dot(ws_ref[g], v_ref[rows, cols]) + bs_ref[g]
            a_ref[rows, cols] = (u_ref[rows, cols].astype(F32) * mixed).astype(BF16)
    ya = _dot(a_ref[...], wa_ref[...])
    yb = _dot(o_ref[...], wb_ref[...])
    merged = ga_ref[...].astype(F32) * ya + gb_ref[...].astype(F32) * yb
    x_mid = x_ref[...] + gt_ref[...] * _dot(merged.astype(BF16), wo_ref[...])
    xmid_ref[...] = x_mid
    h2 = _rms(x_mid) * gffn_ref[...]
    h2 = (h2 * (1.0 + scf_ref[...]) + shf_ref[...]).astype(BF16)
    h2_ref[...] = h2
    scores = jax.nn.sigmoid(_dot(h2, wr_ref[...]))
    biased = scores + br_ref[...]
    e0, e1, w0, w1 = _route(scores.T, biased.T)
    expert = lax.broadcasted_iota(jnp.int32, (LANES, tm), 0)
    comb_t = jnp.where(expert == e0, w0, 0.0) + jnp.where(expert == e1, w1, 0.0)
    comb_ref[...] = comb_t.T


def _post_attention(z, o, x, mod3, mod_base, seq, layer, g_ffn3, w_s, bs_b, w_br_a, w_br_b, w_out,
                    w_router_p, b_router_p, *, tm):
    t, d = x.shape
    tiles_per_seq = seq // tm

    def mod_row(which):
        return lambda i: (mod_base + (i // tiles_per_seq) * N_MOD + which, 0, 0)

    def z_col(col):
        return pl.BlockSpec((tm, d), lambda i: (i, col))

    def layer_mat():
        return pl.BlockSpec((None, d, d), lambda i: (layer, 0, 0))

    row_tile = pl.BlockSpec((tm, d), lambda i: (i, 0))
    return pl.pallas_call(
        _post_kernel,
        out_shape=(jax.ShapeDtypeStruct((t, d), F32),
                   jax.ShapeDtypeStruct((t, d), BF16),
                   jax.ShapeDtypeStruct((t, LANES), F32)),
        grid=(t // tm,),
        in_specs=[
            z_col(COL_U), z_col(COL_V), z_col(COL_GA), z_col(COL_GB),
            row_tile, row_tile,
            pl.BlockSpec((None, 1, d), mod_row(2)),
            pl.BlockSpec((None, 1, d), mod_row(3)),
            pl.BlockSpec((None, 1, d), mod_row(4)),
            pl.BlockSpec((None, 1, d), lambda i: (layer, 0, 0)),
            pl.BlockSpec((None, GM_GROUPS, CHUNK, CHUNK), lambda i: (layer, 0, 0, 0)),
            pl.BlockSpec((None, GM_GROUPS, CHUNK, GROUP_WIDTH), lambda i: (layer, 0, 0, 0)),
            layer_mat(), layer_mat(), layer_mat(),
            pl.BlockSpec((d, LANES), lambda i: (0, 0)),
            pl.BlockSpec((1, LANES), lambda i: (0, 0)),
        ],
        out_specs=(row_tile, row_tile, pl.BlockSpec((tm, LANES), lambda i: (i, 0))),
        scratch_shapes=[pltpu.VMEM((tm, d), BF16)],
        compiler_params=_params(1),
        name="post_attention",
    )(z, z, z, z, o, x, mod3, mod3, mod3, g_ffn3, w_s, bs_b, w_br_a, w_br_b, w_out,
      w_router_p, b_router_p)


def _moe_kernel(h_ref, comb_ref, x_ref, gt_ref, wg_ref, wu_ref, wd_ref, gfin_ref, out_ref, acc_ref,
                *, final_norm):
    e = pl.program_id(1)

    @pl.when(e == 0)
    def _():
        acc_ref[...] = jnp.zeros_like(acc_ref)

    h = h_ref[...]
    gate = _dot(h, wg_ref[...])
    hid = gate * jax.nn.sigmoid(gate) * _dot(h, wu_ref[...])
    comb = comb_ref[...]
    lane = lax.broadcasted_iota(jnp.int32, comb.shape, 1)
    weight = jnp.sum(jnp.where(lane == e, comb, 0.0), axis=-1, keepdims=True)
    acc_ref[...] += _dot((weight * hid).astype(BF16), wd_ref[...])

    @pl.when(e == N_EXPERTS - 1)
    def _():
        x = x_ref[...] + gt_ref[...] * acc_ref[...]
        if final_norm:
            x = _rms(x) * gfin_ref[...]
        out_ref[...] = x


def _moe(h2, comb, x_mid, mod3, mod_base, seq, layer, w_gate, w_up, w_down, g_final2, *, tm, final_norm):
    t, d = x_mid.shape
    tiles_per_seq = seq // tm
    row_tile = pl.BlockSpec((tm, d), lambda i, e: (i, 0))
    return pl.pallas_call(
        functools.partial(_moe_kernel, final_norm=final_norm),
        out_shape=jax.ShapeDtypeStruct((t, d), F32),
        grid=(t // tm, N_EXPERTS),
        in_specs=[
            row_tile,
            pl.BlockSpec((tm, LANES), lambda i, e: (i, 0)),
            row_tile,
            pl.BlockSpec((None, 1, d), lambda i, e: (mod_base + (i // tiles_per_seq) * N_MOD + 5, 0, 0)),
            pl.BlockSpec((None, None, d, D_EXPERT), lambda i, e: (layer, e, 0, 0)),
            pl.BlockSpec((None, None, d, D_EXPERT), lambda i, e: (layer, e, 0, 0)),
            pl.BlockSpec((None, None, D_EXPERT, d), lambda i, e: (layer, e, 0, 0)),
            pl.BlockSpec((1, d), lambda i, e: (0, 0)),
        ],
        out_specs=row_tile,
        scratch_shapes=[pltpu.VMEM((tm, d), F32)],
        compiler_params=_params(2),
        name="moe_experts",
    )(h2, comb, x_mid, mod3, w_gate, w_up, w_down, g_final2)


def _tile(seq, target):
    return min(seq, target)


def _trunk(x3, mod3, batch_offset, padded_batch, w, depth):
    batch, seq, d = x3.shape
    x = x3.reshape(batch * seq, d)
    for layer in range(depth):
        mod_base = (layer * padded_batch + batch_offset) * N_MOD
        z = _in_projection(x, mod3, mod_base, seq, layer, w["g_mix"], w["w_in"], w["g_sg"],
                           tm=_tile(seq, 1024))
        o = _attention(z, batch, seq, layer, w["slopes"], w["lam"], w["g_subln"],
                       tq=_tile(seq, 256), tk=_tile(seq, 512))
        x_mid, h2, comb = _post_attention(z, o, x, mod3, mod_base, seq, layer, w["g_ffn"], w["w_s"],
                                          w["bs_b"], w["w_br_a"], w["w_br_b"], w["w_out"],
                                          w["w_router"], w["b_router"], tm=_tile(seq, 512))
        x = _moe(h2, comb, x_mid, mod3, mod_base, seq, layer, w["w_gate"], w["w_up"], w["w_down"],
                 w["g_final"], tm=_tile(seq, 1024), final_norm=(layer == depth - 1))
    return x.reshape(batch, seq, d)


def kernel(x_prompt, x_sample, c_prompt, c_sample, w_ada, b_ada, g_mix, w_in, g_sg, w_s, b_s, w_br_a,
           lam_q1, lam_k1, lam_q2, lam_k2, g_subln, w_br_b, w_out, g_ffn, w_router, b_router,
           w_gate, w_up, w_down, g_final):
    depth, d = g_mix.shape
    n_prompt, n_sample = c_prompt.shape[0], c_sample.shape[0]
    padded_batch = -(-(n_prompt + n_sample) // 8) * 8
    c_all = jnp.zeros((padded_batch, d), F32)
    c_all = c_all.at[:n_prompt].set(c_prompt).at[n_prompt:n_prompt + n_sample].set(c_sample)
    mod = _modulations(c_all, w_ada, b_ada)
    mod3 = mod.reshape(depth * padded_batch * N_MOD, 1, d)

    slopes = jnp.exp2(-(8.0 / N_HEADS) * jnp.arange(1, N_HEADS + 1, dtype=F32))
    w = {
        "g_mix": g_mix.reshape(depth, 1, d),
        "w_in": w_in.astype(BF16),
        "g_sg": g_sg.reshape(depth, 1, d),
        "slopes": slopes.reshape(N_HEADS, 1, 1),
        "lam": tuple(p.reshape(depth, 1, HEAD_DIM) for p in (lam_q1, lam_k1, lam_q2, lam_k2)),
        "g_subln": g_subln.reshape(depth, 1, HEAD_WIDTH),
        "g_ffn": g_ffn.reshape(depth, 1, d),
        "w_s": w_s.astype(BF16),
        "bs_b": jnp.broadcast_to(b_s[..., None], b_s.shape + (GROUP_WIDTH,)),
        "w_br_a": w_br_a.astype(BF16),
        "w_br_b": w_br_b.astype(BF16),
        "w_out": w_out.astype(BF16),
        "w_router": jnp.pad(w_router, ((0, 0), (0, LANES - N_EXPERTS))).astype(BF16),
        "b_router": jnp.pad(b_router, (0, LANES - N_EXPERTS)).reshape(1, LANES),
        "w_gate": w_gate.astype(BF16),
        "w_up": w_up.astype(BF16),
        "w_down": w_down.astype(BF16),
        "g_final": g_final.reshape(1, d),
    }
    y_prompt = _trunk(x_prompt, mod3, 0, padded_batch, w, depth)
    y_sample = _trunk(x_sample, mod3, n_prompt, padded_batch, w, depth)
    return (y_prompt, y_sample)
```

```python
import functools
import math

import jax
import jax.numpy as jnp
from jax import lax
from jax.experimental import pallas as pl
from jax.experimental.pallas import tpu as pltpu

D_MODEL = 1024
N_HEADS = 8
HEAD_DIM = 64
HEAD_WIDTH = 2 * HEAD_DIM
CHUNK = 128
GM_GROUPS = 8
GROUP_WIDTH = D_MODEL // GM_GROUPS
N_EXPERTS = 16
N_GROUPS = 4
EXPERTS_PER_GROUP = N_EXPERTS // N_GROUPS
D_EXPERT = 512
N_MOD = 6
EPS = 1e-6
COL_U, COL_V, COL_Q, COL_K, COL_VA, COL_GA, COL_GB = range(7)
N_COL_BLOCKS = 7
LANES = 128
NEG_BIG = -1e30

F32 = jnp.float32
BF16 = jnp.bfloat16

VMEM_LIMIT = 56 * 1024 * 1024


def _params(n_axes):
    return pltpu.CompilerParams(dimension_semantics=("arbitrary",) * n_axes,
                                vmem_limit_bytes=VMEM_LIMIT)


def _lambda_init(layer):
    return 0.8 - 0.6 * math.exp(-0.3 * layer)


def _dot(a, b):
    return jnp.dot(a, b, preferred_element_type=F32)


def _mod_kernel(c_ref, w_ref, b_ref, o_ref):
    c = c_ref[...]
    a = c * jax.nn.sigmoid(c)
    o_ref[...] = _dot(a.astype(BF16), w_ref[...].astype(BF16)) + b_ref[...]


def _modulations(c_all, w_ada, b_ada):
    depth, d, n = w_ada.shape
    bp = c_all.shape[0]
    tn = n // 4
    return pl.pallas_call(
        _mod_kernel,
        out_shape=jax.ShapeDtypeStruct((depth, bp, n), F32),
        grid=(depth, n // tn),
        in_specs=[
            pl.BlockSpec((bp, d), lambda l, j: (0, 0)),
            pl.BlockSpec((None, d, tn), lambda l, j: (l, 0, j)),
            pl.BlockSpec((None, 1, tn), lambda l, j: (l, 0, j)),
        ],
        out_specs=pl.BlockSpec((None, bp, tn), lambda l, j: (l, 0, j)),
        compiler_params=_params(2),
        name="adaln_mod",
    )(c_all, w_ada, b_ada.reshape(depth, 1, n))


def _rms(x):
    return x * lax.rsqrt(jnp.mean(x * x, axis=-1, keepdims=True) + EPS)


def _in_kernel(x_ref, sh_ref, sc_ref, g_ref, w_ref, gsg_ref, z_ref, h_ref, *, rows):
    j = pl.program_id(1)
    tm = x_ref.shape[0]
    n_chunks = tm // rows

    @pl.when(j == 0)
    def _():
        for r in range(n_chunks):
            sl = pl.ds(r * rows, rows)
            h = _rms(x_ref[sl, :]) * g_ref[...]
            h = h * (1.0 + sc_ref[...]) + sh_ref[...]
            h_ref[sl, :] = h.astype(BF16)

    def project(epilogue):
        for r in range(n_chunks):
            sl = pl.ds(r * rows, rows)
            z_ref[sl, :] = epilogue(_dot(h_ref[sl, :], w_ref[...])).astype(BF16)

    @pl.when(j == COL_U)
    def _():
        project(jax.nn.gelu)

    @pl.when(j == COL_V)
    def _():
        project(lambda t: _rms(jax.nn.gelu(t)) * gsg_ref[...])

    @pl.when(j == COL_Q)
    def _():
        project(lambda t: t * (HEAD_DIM ** -0.5))

    @pl.when((j == COL_K) | (j == COL_VA))
    def _():
        project(lambda t: t)

    @pl.when((j == COL_GA) | (j == COL_GB))
    def _():
        project(jax.nn.sigmoid)


def _in_projection(x, mod3, mod_base, seq, layer, g_mix3, w_in, g_sg3, *, tm):
    t, d = x.shape
    tiles_per_seq = seq // tm

    def mod_row(which):
        return lambda i, j: (mod_base + (i // tiles_per_seq) * N_MOD + which, 0, 0)

    return pl.pallas_call(
        functools.partial(_in_kernel, rows=min(tm, 256)),
        out_shape=jax.ShapeDtypeStruct((t, N_COL_BLOCKS * d), BF16),
        grid=(t // tm, N_COL_BLOCKS),
        in_specs=[
            pl.BlockSpec((tm, d), lambda i, j: (i, 0)),
            pl.BlockSpec((None, 1, d), mod_row(0)),
            pl.BlockSpec((None, 1, d), mod_row(1)),
            pl.BlockSpec((None, 1, d), lambda i, j: (layer, 0, 0)),
            pl.BlockSpec((None, d, d), lambda i, j: (layer, 0, j)),
            pl.BlockSpec((None, 1, d), lambda i, j: (layer, 0, 0)),
        ],
        out_specs=pl.BlockSpec((tm, d), lambda i, j: (i, j)),
        scratch_shapes=[pltpu.VMEM((tm, d), BF16)],
        compiler_params=_params(2),
        name="in_projection",
    )(x, mod3, mod3, g_mix3, w_in, g_sg3)


SCORE_BOUND_SLACK = 1.01
SCORE_BOUND_PAD = 1e-3
EXP_UNDERFLOW = 105.0
SAFE_SCORE_BOUND = 28.0
POS_SPLIT = 256


def _plan_kernel(q_ref, k_ref, sel_ref, slope_ref, r_ref, m_ref, kmax_ref, qn_ref, sf_ref, kn_ref, *, tq):
    t = pl.program_id(1)
    tp = q_ref.shape[0]
    q = q_ref[...].astype(F32)
    k = k_ref[...].astype(F32)

    def group_sums(x):
        hi = x.astype(BF16)
        lo = (x - hi.astype(F32)).astype(BF16)
        return _dot(hi, sel_ref[...]) + _dot(lo, sel_ref[...])

    rows = pl.ds(pl.multiple_of(t * tp, tp), tp)
    qn_ref[rows, :] = group_sums(q * q)
    sf_ref[rows, :] = group_sums(q * k)
    kn = jnp.max(group_sums(k * k), axis=0, keepdims=True)

    @pl.when(t == 0)
    def _():
        kn_ref[...] = jnp.zeros_like(kn_ref)

    kn_ref[...] = jnp.maximum(kn_ref[...], kn)

    @pl.when(t == pl.num_programs(1) - 1)
    def _():
        kmax = jnp.sqrt(kn_ref[...])
        kmax_ref[...] = jnp.broadcast_to(kmax, kmax_ref.shape)

        def tile(i, carry):
            tile_rows = pl.ds(pl.multiple_of(i * tq, tq), tq)
            m = jnp.sqrt(qn_ref[tile_rows, :]) * kmax * SCORE_BOUND_SLACK + SCORE_BOUND_PAD
            r = (m - sf_ref[tile_rows, :] + EXP_UNDERFLOW) / slope_ref[...]
            m_ref[pl.ds(i, 1), :] = jnp.max(m, axis=0, keepdims=True)
            r_ref[pl.ds(i, 1), :] = jnp.max(r, axis=0, keepdims=True)
            return carry

        lax.fori_loop(0, qn_ref.shape[0] // tq, tile, 0)


def _attention_plan(z, batch, seq, slopes, *, tq, tk):
    n_q, n_k = seq // tq, seq // tk
    tp = min(seq, 1024)
    col = jnp.arange(D_MODEL) // HEAD_DIM
    sel = (col[:, None] == jnp.arange(LANES)[None, :]).astype(BF16)
    slope_row = jnp.ones((LANES,), F32).at[:2 * N_HEADS].set(jnp.repeat(slopes, 2)).reshape(1, LANES)
    stat = jax.ShapeDtypeStruct((batch, n_q, LANES), F32)
    r, m, kmax = pl.pallas_call(
        functools.partial(_plan_kernel, tq=tq),
        out_shape=(stat, stat, jax.ShapeDtypeStruct((batch, 8, LANES), F32)),
        grid=(batch, seq // tp),
        in_specs=[
            pl.BlockSpec((tp, D_MODEL), lambda b, t: (b * (seq // tp) + t, COL_Q)),
            pl.BlockSpec((tp, D_MODEL), lambda b, t: (b * (seq // tp) + t, COL_K)),
            pl.BlockSpec((D_MODEL, LANES), lambda b, t: (0, 0)),
            pl.BlockSpec((1, LANES), lambda b, t: (0, 0)),
        ],
        out_specs=(pl.BlockSpec((None, n_q, LANES), lambda b, t: (b, 0, 0)),
                   pl.BlockSpec((None, n_q, LANES), lambda b, t: (b, 0, 0)),
                   pl.BlockSpec((None, 8, LANES), lambda b, t: (b, 0, 0))),
        scratch_shapes=[pltpu.VMEM((seq, LANES), F32), pltpu.VMEM((seq, LANES), F32),
                        pltpu.VMEM((1, LANES), F32)],
        compiler_params=_params(2),
        name="attention_plan",
    )(z, z, sel, slope_row)

    def per_head(a):
        return jnp.max(a[..., :2 * N_HEADS].reshape(batch, n_q, N_HEADS, 2), axis=-1).transpose(0, 2, 1)

    r, m = per_head(r), per_head(m)
    r = jnp.where(r == r, r, jnp.inf)
    q0 = (jnp.arange(n_q) * tq).astype(F32)[None, None, :]
    diag = (jnp.arange(n_q) * tq // tk)[None, None, :]
    lo = jnp.clip(jnp.ceil((q0 + 1.0 - r) / tk - 1.0), 0, n_k).astype(jnp.int32)
    hi = jnp.clip(jnp.floor((r + q0 + (tq - 1.0)) / tk), 0, n_k - 1).astype(jnp.int32)
    lo = jnp.minimum(lo, diag)
    hi = jnp.maximum(hi, diag)
    safe = (m <= SAFE_SCORE_BOUND).astype(jnp.int32)
    head_stat = jnp.zeros((batch, N_HEADS, 1, LANES), F32)
    head_stat = head_stat.at[:, :, 0, 0].set(jnp.broadcast_to(slopes, (batch, N_HEADS)))
    head_stat = head_stat.at[:, :, 0, 1].set(kmax[:, 0, 0:2 * N_HEADS:2])
    head_stat = head_stat.at[:, :, 0, 2].set(kmax[:, 0, 1:2 * N_HEADS:2])
    return lo.reshape(-1), hi.reshape(-1), safe.reshape(-1), head_stat


def _key_position_features(seq):
    pos = jnp.arange(seq)
    feat = jnp.zeros((seq, LANES), F32)
    feat = feat.at[:, 0].set((pos // POS_SPLIT * POS_SPLIT).astype(F32))
    feat = feat.at[:, 1].set((pos % POS_SPLIT).astype(F32))
    feat = feat.at[:, 2:5].set(1.0)
    return feat.astype(BF16)


def _attn_kernel(lo_ref, hi_ref, safe_ref, q_ref, k_ref, v_ref, kf_ref, hs_ref,
                 lq1_ref, lk1_ref, lq2_ref, lk2_ref, gsub_ref,
                 o_ref, vt_ref, acc0_ref, acc1_ref, l0_ref, l1_ref, p_ref, *, tk, lam_init):
    b, h, i = pl.program_id(0), pl.program_id(1), pl.program_id(2)
    plan = (b * N_HEADS + h) * pl.num_programs(2) + i
    tq = q_ref.shape[0]
    n_k = k_ref.shape[0] // tk

    @pl.when(i == 0)
    def _():
        for j in range(n_k):
            vt_ref[j] = v_ref[pl.ds(j * tk, tk), :].T

    q = q_ref[...]
    lane = lax.broadcasted_iota(jnp.int32, q.shape, 1)
    zero = jnp.zeros_like(q)
    qt0 = jnp.where(lane < HEAD_DIM, q, zero).T
    qt1 = jnp.where(lane >= HEAD_DIM, q, zero).T
    head_stat = hs_ref[...]
    slope = head_stat[:, 0:1]
    q_start = i * tq
    rel = (lax.broadcasted_iota(jnp.int32, (tk, tq), 0)
           - lax.broadcasted_iota(jnp.int32, (tk, tq), 1)).astype(F32)

    def alibi(j):
        return slope * jnp.abs(rel + (j * tk - q_start).astype(F32))

    def key_rows(j):
        return pl.ds(pl.multiple_of(j * tk, tk), tk)

    acc0_ref[...] = jnp.zeros_like(acc0_ref)
    acc1_ref[...] = jnp.zeros_like(acc1_ref)
    l0_ref[...] = jnp.zeros_like(l0_ref)
    l1_ref[...] = jnp.zeros_like(l1_ref)

    @pl.when(safe_ref[plan] == 1)
    def _():
        qpos = q_start + lax.broadcasted_iota(jnp.int32, (1, tq), 1)
        q_hi = (qpos // POS_SPLIT * POS_SPLIT).astype(F32)
        q_lo = (qpos % POS_SPLIT).astype(F32)
        row = lax.broadcasted_iota(jnp.int32, (HEAD_WIDTH, tq), 0)

        def query_features(sign, bound):
            feat = jnp.where(row < 2, sign * slope,
                             jnp.where(row == 2, -sign * slope * q_hi,
                                       jnp.where(row == 3, -sign * slope * q_lo, 0.0)))
            return jnp.where(row == 4, -bound, feat).astype(BF16)

        def score_bound(qt, kmax):
            qf = qt.astype(F32)
            norm = jnp.sqrt(jnp.sum(qf * qf, axis=0, keepdims=True))
            return norm * kmax * SCORE_BOUND_SLACK + SCORE_BOUND_PAD

        m0 = score_bound(qt0, head_stat[:, 1:2])
        m1 = score_bound(qt1, head_stat[:, 2:3])

        def operands(sign):
            return (jnp.concatenate([qt0, query_features(sign, m0)], axis=0),
                    jnp.concatenate([qt1, query_features(sign, m1)], axis=0))

        def probabilities(s, l_ref):
            p = jnp.exp(s)
            l_ref[...] += jnp.sum(p.reshape(tk // 8, 8, tq), axis=0)
            return p.astype(BF16)

        def tile_probabilities(j, qa0, qa1, explicit_bias):
            ka = jnp.concatenate([k_ref[key_rows(j), :], kf_ref[key_rows(j), :]], axis=1)
            s0, s1 = _dot(ka, qa0), _dot(ka, qa1)
            if explicit_bias:
                bias = alibi(j)
                s0, s1 = s0 - bias, s1 - bias
            return probabilities(s0, l0_ref), probabilities(s1, l1_ref)

        def accumulate(j, slot):
            vt = vt_ref[j]
            acc0_ref[...] += _dot(vt, p_ref[slot, 0])
            acc1_ref[...] += _dot(vt, p_ref[slot, 1])

        def stash(slot, p0, p1):
            p_ref[slot, 0] = p0
            p_ref[slot, 1] = p1

        def run(first, last, sign, carry):
            qa0, qa1 = operands(sign)

            def body(j, carry):
                pending, slot = carry
                p0, p1 = tile_probabilities(j, qa0, qa1, False)
                accumulate(pending, slot)
                stash(1 - slot, p0, p1)
                return j, 1 - slot

            return lax.fori_loop(first, last, body, carry)

        diag = q_start // tk
        stash(0, *tile_probabilities(diag, *operands(0.0), True))
        carry = run(lo_ref[plan], diag, 1.0, (diag, 0))
        carry = run(diag + 1, hi_ref[plan] + 1, -1.0, carry)
        accumulate(*carry)

    @pl.when(safe_ref[plan] == 0)
    def _():
        def update(s, m, l, acc_ref, vt):
            m_new = jnp.maximum(m, jnp.max(s, axis=0, keepdims=True))
            alpha = jnp.exp(m - m_new)
            p = jnp.exp(s - m_new)
            l_new = alpha * l + jnp.sum(p, axis=0, keepdims=True)
            acc_ref[...] = alpha * acc_ref[...] + _dot(vt, p.astype(BF16))
            return m_new, l_new

        def body(j, carry):
            m0, l0, m1, l1 = carry
            kt = k_ref[key_rows(j), :]
            bias = alibi(j)
            vt = vt_ref[j]
            m0, l0 = update(_dot(kt, qt0) - bias, m0, l0, acc0_ref, vt)
            m1, l1 = update(_dot(kt, qt1) - bias, m1, l1, acc1_ref, vt)
            return m0, l0, m1, l1

        init_m = jnp.full((1, tq), NEG_BIG, F32)
        init_l = jnp.zeros((1, tq), F32)
        _, l0, _, l1 = lax.fori_loop(0, n_k, body, (init_m, init_l, init_m, init_l))
        l0_ref[0:1, :] = l0
        l1_ref[0:1, :] = l1

    l0 = jnp.sum(l0_ref[...], axis=0, keepdims=True)
    l1 = jnp.sum(l1_ref[...], axis=0, keepdims=True)
    lam = (jnp.exp(jnp.sum(lq1_ref[...] * lk1_ref[...], axis=-1, keepdims=True))
           - jnp.exp(jnp.sum(lq2_ref[...] * lk2_ref[...], axis=-1, keepdims=True)) + lam_init)
    out_t = acc0_ref[...] / l0 - lam * (acc1_ref[...] / l1)
    out = _rms(out_t.T) * gsub_ref[...] * (1.0 - lam_init)
    o_ref[...] = out.astype(BF16)


def _attention(z, batch, seq, layer, slopes, lam_params, g_subln3, *, tq, tk):
    t = z.shape[0]
    n_q = seq // tq
    blocks_per_col = D_MODEL // HEAD_WIDTH
    lo, hi, safe, head_stat = _attention_plan(z, batch, seq, slopes, tq=tq, tk=tk)
    lam_spec = pl.BlockSpec((None, 1, HEAD_DIM), lambda b, h, i, *_: (layer, 0, 0))
    grid_spec = pltpu.PrefetchScalarGridSpec(
        num_scalar_prefetch=3,
        grid=(batch, N_HEADS, n_q),
        in_specs=[
            pl.BlockSpec((tq, HEAD_WIDTH), lambda b, h, i, *_: (b * n_q + i, COL_Q * blocks_per_col + h)),
            pl.BlockSpec((seq, HEAD_WIDTH), lambda b, h, i, *_: (b, COL_K * blocks_per_col + h)),
            pl.BlockSpec((seq, HEAD_WIDTH), lambda b, h, i, *_: (b, COL_VA * blocks_per_col + h)),
            pl.BlockSpec((seq, LANES), lambda b, h, i, *_: (0, 0)),
            pl.BlockSpec((None, None, 1, LANES), lambda b, h, i, *_: (b, h, 0, 0)),
            lam_spec, lam_spec, lam_spec, lam_spec,
            pl.BlockSpec((None, 1, HEAD_WIDTH), lambda b, h, i, *_: (layer, 0, 0)),
        ],
        out_specs=pl.BlockSpec((tq, HEAD_WIDTH), lambda b, h, i, *_: (b * n_q + i, h)),
        scratch_shapes=[
            pltpu.VMEM((seq // tk, HEAD_WIDTH, tk), BF16),
            pltpu.VMEM((HEAD_WIDTH, tq), F32),
            pltpu.VMEM((HEAD_WIDTH, tq), F32),
            pltpu.VMEM((8, tq), F32),
            pltpu.VMEM((8, tq), F32),
            pltpu.VMEM((2, 2, tk, tq), BF16),
        ],
    )
    return pl.pallas_call(
        functools.partial(_attn_kernel, tk=tk, lam_init=_lambda_init(layer)),
        out_shape=jax.ShapeDtypeStruct((t, D_MODEL), BF16),
        grid_spec=grid_spec,
        compiler_params=_params(3),
        name="diff_attention",
    )(lo, hi, safe, z, z, z, _key_position_features(seq), head_stat, *lam_params, g_subln3)


def _route(scores_t, biased_t):
    def row(a, e):
        return a[e:e + 1, :]

    grp_scores = []
    for g in range(N_GROUPS):
        a, b, c, d = (row(biased_t, g * EXPERTS_PER_GROUP + k) for k in range(EXPERTS_PER_GROUP))
        hi_ab, lo_ab = jnp.maximum(a, b), jnp.minimum(a, b)
        hi_cd, lo_cd = jnp.maximum(c, d), jnp.minimum(c, d)
        top1 = jnp.maximum(hi_ab, hi_cd)
        top2 = jnp.maximum(jnp.minimum(hi_ab, hi_cd), jnp.maximum(lo_ab, lo_cd))
        grp_scores.append(top1 + top2)
    best = grp_scores[0]
    sel = jnp.zeros_like(best, dtype=jnp.int32)
    for g in range(1, N_GROUPS):
        better = grp_scores[g] > best
        best = jnp.where(better, grp_scores[g], best)
        sel = jnp.where(better, g, sel)

    def pick(a, k):
        out = row(a, k)
        for g in range(1, N_GROUPS):
            out = jnp.where(sel == g, row(a, g * EXPERTS_PER_GROUP + k), out)
        return out

    cand_b = [pick(biased_t, k) for k in range(EXPERTS_PER_GROUP)]
    cand_s = [pick(scores_t, k) for k in range(EXPERTS_PER_GROUP)]
    v0, k0, w0 = cand_b[0], jnp.zeros_like(sel), cand_s[0]
    for k in range(1, EXPERTS_PER_GROUP):
        better = cand_b[k] > v0
        v0 = jnp.where(better, cand_b[k], v0)
        k0 = jnp.where(better, k, k0)
        w0 = jnp.where(better, cand_s[k], w0)
    v1 = jnp.full_like(v0, -jnp.inf)
    k1 = jnp.zeros_like(sel)
    w1 = jnp.zeros_like(w0)
    for k in range(EXPERTS_PER_GROUP):
        better = (k0 != k) & (cand_b[k] > v1)
        v1 = jnp.where(better, cand_b[k], v1)
        k1 = jnp.where(better, k, k1)
        w1 = jnp.where(better, cand_s[k], w1)
    total = w0 + w1
    e0 = sel * EXPERTS_PER_GROUP + k0
    e1 = sel * EXPERTS_PER_GROUP + k1
    return e0, e1, w0 / total, w1 / total


def _post_kernel(u_ref, v_ref, ga_ref, gb_ref, o_ref, x_ref, gt_ref, shf_ref, scf_ref, gffn_ref,
                 ws_ref, bs_ref, wa_ref, wb_ref, wo_ref, wr_ref, br_ref,
                 xmid_ref, h2_ref, comb_ref, a_ref):
    tm = x_ref.shape[0]
    for n in range(tm // CHUNK):
        rows = pl.ds(n * CHUNK, CHUNK)
        for g in range(GM_GROUPS):
            cols = pl.ds(g * GROUP_WIDTH, GROUP_WIDTH)
            mixed = _dot(ws_ref[g], v_ref[rows, cols]) + bs_ref[g]
            a_ref[rows, cols] = (u_ref[rows, cols].astype(F32) * mixed).astype(BF16)
    ya = _dot(a_ref[...], wa_ref[...])
    yb = _dot(o_ref[...], wb_ref[...])
    merged = ga_ref[...].astype(F32) * ya + gb_ref[...].astype(F32) * yb
    x_mid = x_ref[...] + gt_ref[...] * _dot(merged.astype(BF16), wo_ref[...])
    xmid_ref[...] = x_mid
    h2 = _rms(x_mid) * gffn_ref[...]
    h2 = (h2 * (1.0 + scf_ref[...]) + shf_ref[...]).astype(BF16)
    h2_ref[...] = h2
    scores = jax.nn.sigmoid(_dot(h2, wr_ref[...]))
    biased = scores + br_ref[...]
    e0, e1, w0, w1 = _route(scores.T, biased.T)
    expert = lax.broadcasted_iota(jnp.int32, (LANES, tm), 0)
    comb_t = jnp.where(expert == e0, w0, 0.0) + jnp.where(expert == e1, w1, 0.0)
    comb_ref[...] = comb_t.T


def _post_attention(z, o, x, mod3, mod_base, seq, layer, g_ffn3, w_s, bs_b, w_br_a, w_br_b, w_out,
                    w_router_p, b_router_p, *, tm):
    t, d = x.shape
    tiles_per_seq = seq // tm

    def mod_row(which):
        return lambda i: (mod_base + (i // tiles_per_seq) * N_MOD + which, 0, 0)

    def z_col(col):
        return pl.BlockSpec((tm, d), lambda i: (i, col))

    def layer_mat():
        return pl.BlockSpec((None, d, d), lambda i: (layer, 0, 0))

    row_tile = pl.BlockSpec((tm, d), lambda i: (i, 0))
    return pl.pallas_call(
        _post_kernel,
        out_shape=(jax.ShapeDtypeStruct((t, d), F32),
                   jax.ShapeDtypeStruct((t, d), BF16),
                   jax.ShapeDtypeStruct((t, LANES), F32)),
        grid=(t // tm,),
        in_specs=[
            z_col(COL_U), z_col(COL_V), z_col(COL_GA), z_col(COL_GB),
            row_tile, row_tile,
            pl.BlockSpec((None, 1, d), mod_row(2)),
            pl.BlockSpec((None, 1, d), mod_row(3)),
            pl.BlockSpec((None, 1, d), mod_row(4)),
            pl.BlockSpec((None, 1, d), lambda i: (layer, 0, 0)),
            pl.BlockSpec((None, GM_GROUPS, CHUNK, CHUNK), lambda i: (layer, 0, 0, 0)),
            pl.BlockSpec((None, GM_GROUPS, CHUNK, GROUP_WIDTH), lambda i: (layer, 0, 0, 0)),
            layer_mat(), layer_mat(), layer_mat(),
            pl.BlockSpec((d, LANES), lambda i: (0, 0)),
            pl.BlockSpec((1, LANES), lambda i: (0, 0)),
        ],
        out_specs=(row_tile, row_tile, pl.BlockSpec((tm, LANES), lambda i: (i, 0))),
        scratch_shapes=[pltpu.VMEM((tm, d), BF16)],
        compiler_params=_params(1),
        name="post_attention",
    )(z, z, z, z, o, x, mod3, mod3, mod3, g_ffn3, w_s, bs_b, w_br_a, w_br_b, w_out,
      w_router_p, b_router_p)


def _moe_kernel(h_ref, comb_ref, x_ref, gt_ref, wg_ref, wu_ref, wd_ref, gfin_ref, out_ref, acc_ref,
                *, final_norm):
    e = pl.program_id(1)

    @pl.when(e == 0)
    def _():
        acc_ref[...] = jnp.zeros_like(acc_ref)

    h = h_ref[...]
    gate = _dot(h, wg_ref[...])
    hid = gate * jax.nn.sigmoid(gate) * _dot(h, wu_ref[...])
    comb = comb_ref[...]
    lane = lax.broadcasted_iota(jnp.int32, comb.shape, 1)
    weight = jnp.sum(jnp.where(lane == e, comb, 0.0), axis=-1, keepdims=True)
    acc_ref[...] += _dot((weight * hid).astype(BF16), wd_ref[...])

    @pl.when(e == N_EXPERTS - 1)
    def _():
        x = x_ref[...] + gt_ref[...] * acc_ref[...]
        if final_norm:
            x = _rms(x) * gfin_ref[...]
        out_ref[...] = x


def _moe(h2, comb, x_mid, mod3, mod_base, seq, layer, w_gate, w_up, w_down, g_final2, *, tm, final_norm):
    t, d = x_mid.shape
    tiles_per_seq = seq // tm
    row_tile = pl.BlockSpec((tm, d), lambda i, e: (i, 0))
    return pl.pallas_call(
        functools.partial(_moe_kernel, final_norm=final_norm),
        out_shape=jax.ShapeDtypeStruct((t, d), F32),
        grid=(t // tm, N_EXPERTS),
        in_specs=[
            row_tile,
            pl.BlockSpec((tm, LANES), lambda i, e: (i, 0)),
            row_tile,
            pl.BlockSpec((None, 1, d), lambda i, e: (mod_base + (i // tiles_per_seq) * N_MOD + 5, 0, 0)),
            pl.BlockSpec((None, None, d, D_EXPERT), lambda i, e: (layer, e, 0, 0)),
            pl.BlockSpec((None, None, d, D_EXPERT), lambda i, e: (layer, e, 0, 0)),
            pl.BlockSpec((None, None, D_EXPERT, d), lambda i, e: (layer, e, 0, 0)),
            pl.BlockSpec((1, d), lambda i, e: (0, 0)),
        ],
        out_specs=row_tile,
        scratch_shapes=[pltpu.VMEM((tm, d), F32)],
        compiler_params=_params(2),
        name="moe_experts",
    )(h2, comb, x_mid, mod3, w_gate, w_up, w_down, g_final2)


def _tile(seq, target):
    return min(seq, target)


def _trunk(x3, mod3, batch_offset, padded_batch, w, depth):
    batch, seq, d = x3.shape
    x = x3.reshape(batch * seq, d)
    for layer in range(depth):
        mod_base = (layer * padded_batch + batch_offset) * N_MOD
        z = _in_projection(x, mod3, mod_base, seq, layer, w["g_mix"], w["w_in"], w["g_sg"],
                           tm=_tile(seq, 1024))
        o = _attention(z, batch, seq, layer, w["slopes"], w["lam"], w["g_subln"],
                       tq=_tile(seq, 512), tk=_tile(seq, 512))
        x_mid, h2, comb = _post_attention(z, o, x, mod3, mod_base, seq, layer, w["g_ffn"], w["w_s"],
                                          w["bs_b"], w["w_br_a"], w["w_br_b"], w["w_out"],
                                          w["w_router"], w["b_router"], tm=_tile(seq, 512))
        x = _moe(h2, comb, x_mid, mod3, mod_base, seq, layer, w["w_gate"], w["w_up"], w["w_down"],
                 w["g_final"], tm=_tile(seq, 1024), final_norm=(layer == depth - 1))
    return x.reshape(batch, seq, d)


def kernel(x_prompt, x_sample, c_prompt, c_sample, w_ada, b_ada, g_mix, w_in, g_sg, w_s, b_s, w_br_a,
           lam_q1, lam_k1, lam_q2, lam_k2, g_subln, w_br_b, w_out, g_ffn, w_router, b_router,
           w_gate, w_up, w_down, g_final):
    depth, d = g_mix.shape
    n_prompt, n_sample = c_prompt.shape[0], c_sample.shape[0]
    padded_batch = -(-(n_prompt + n_sample) // 8) * 8
    c_all = jnp.zeros((padded_batch, d), F32)
    c_all = c_all.at[:n_prompt].set(c_prompt).at[n_prompt:n_prompt + n_sample].set(c_sample)
    mod = _modulations(c_all, w_ada, b_ada)
    mod3 = mod.reshape(depth * padded_batch * N_MOD, 1, d)

    slopes = jnp.exp2(-(8.0 / N_HEADS) * jnp.arange(1, N_HEADS + 1, dtype=F32))
    w = {
        "g_mix": g_mix.reshape(depth, 1, d),
        "w_in": w_in.astype(BF16),
        "g_sg": g_sg.reshape(depth, 1, d),
        "slopes": slopes,
        "lam": tuple(p.reshape(depth, 1, HEAD_DIM) for p in (lam_q1, lam_k1, lam_q2, lam_k2)),
        "g_subln": g_subln.reshape(depth, 1, HEAD_WIDTH),
        "g_ffn": g_ffn.reshape(depth, 1, d),
        "w_s": w_s.astype(BF16),
        "bs_b": jnp.broadcast_to(b_s[..., None], b_s.shape + (GROUP_WIDTH,)),
        "w_br_a": w_br_a.astype(BF16),
        "w_br_b": w_br_b.astype(BF16),
        "w_out": w_out.astype(BF16),
        "w_router": jnp.pad(w_router, ((0, 0), (0, LANES - N_EXPERTS))).astype(BF16),
        "b_router": jnp.pad(b_router, (0, LANES - N_EXPERTS)).reshape(1, LANES),
        "w_gate": w_gate.astype(BF16),
        "w_up": w_up.astype(BF16),
        "w_down": w_down.astype(BF16),
        "g_final": g_final.reshape(1, d),
    }
    y_prompt = _trunk(x_prompt, mod3, 0, padded_batch, w, depth)
    y_sample = _trunk(x_sample, mod3, n_prompt, padded_batch, w, depth)
    return (y_prompt, y_sample)
```

```python
import functools
import math

import jax
import jax.numpy as jnp
from jax import lax
from jax.experimental import pallas as pl
from jax.experimental.pallas import tpu as pltpu

D_MODEL = 1024
N_HEADS = 8
HEAD_DIM = 64
HEAD_WIDTH = 2 * HEAD_DIM
CHUNK = 128
GM_GROUPS = 8
GROUP_WIDTH = D_MODEL // GM_GROUPS
N_EXPERTS = 16
N_GROUPS = 4
EXPERTS_PER_GROUP = N_EXPERTS // N_GROUPS
D_EXPERT = 512
N_MOD = 6
EPS = 1e-6
COL_U, COL_V, COL_Q, COL_K, COL_VA, COL_GA, COL_GB = range(7)
N_COL_BLOCKS = 7
LANES = 128
NEG_BIG = -1e30

F32 = jnp.float32
BF16 = jnp.bfloat16

VMEM_LIMIT = 56 * 1024 * 1024


def _params(n_axes):
    return pltpu.CompilerParams(dimension_semantics=("arbitrary",) * n_axes,
                                vmem_limit_bytes=VMEM_LIMIT)


def _lambda_init(layer):
    return 0.8 - 0.6 * math.exp(-0.3 * layer)


def _dot(a, b):
    return jnp.dot(a, b, preferred_element_type=F32)


def _mod_kernel(c_ref, w_ref, b_ref, o_ref):
    c = c_ref[...]
    a = c * jax.nn.sigmoid(c)
    o_ref[...] = _dot(a.astype(BF16), w_ref[...].astype(BF16)) + b_ref[...]


def _modulations(c_all, w_ada, b_ada):
    depth, d, n = w_ada.shape
    bp = c_all.shape[0]
    tn = n // 4
    return pl.pallas_call(
        _mod_kernel,
        out_shape=jax.ShapeDtypeStruct((depth, bp, n), F32),
        grid=(depth, n // tn),
        in_specs=[
            pl.BlockSpec((bp, d), lambda l, j: (0, 0)),
            pl.BlockSpec((None, d, tn), lambda l, j: (l, 0, j)),
            pl.BlockSpec((None, 1, tn), lambda l, j: (l, 0, j)),
        ],
        out_specs=pl.BlockSpec((None, bp, tn), lambda l, j: (l, 0, j)),
        compiler_params=_params(2),
        name="adaln_mod",
    )(c_all, w_ada, b_ada.reshape(depth, 1, n))


def _rms(x):
    return x * lax.rsqrt(jnp.mean(x * x, axis=-1, keepdims=True) + EPS)


def _in_kernel(x_ref, sh_ref, sc_ref, g_ref, w_ref, gsg_ref, z_ref, h_ref, *, rows):
    j = pl.program_id(1)
    tm = x_ref.shape[0]
    n_chunks = tm // rows

    @pl.when(j == 0)
    def _():
        for r in range(n_chunks):
            sl = pl.ds(r * rows, rows)
            h = _rms(x_ref[sl, :]) * g_ref[...]
            h = h * (1.0 + sc_ref[...]) + sh_ref[...]
            h_ref[sl, :] = h.astype(BF16)

    def project(epilogue):
        for r in range(n_chunks):
            sl = pl.ds(r * rows, rows)
            z_ref[sl, :] = epilogue(_dot(h_ref[sl, :], w_ref[...])).astype(BF16)

    @pl.when(j == COL_U)
    def _():
        project(jax.nn.gelu)

    @pl.when(j == COL_V)
    def _():
        project(lambda t: _rms(jax.nn.gelu(t)) * gsg_ref[...])

    @pl.when(j == COL_Q)
    def _():
        project(lambda t: t * (HEAD_DIM ** -0.5))

    @pl.when((j == COL_K) | (j == COL_VA))
    def _():
        project(lambda t: t)

    @pl.when((j == COL_GA) | (j == COL_GB))
    def _():
        project(jax.nn.sigmoid)


def _in_projection(x, mod3, mod_base, seq, layer, g_mix3, w_in, g_sg3, *, tm):
    t, d = x.shape
    tiles_per_seq = seq // tm

    def mod_row(which):
        return lambda i, j: (mod_base + (i // tiles_per_seq) * N_MOD + which, 0, 0)

    return pl.pallas_call(
        functools.partial(_in_kernel, rows=min(tm, 256)),
        out_shape=jax.ShapeDtypeStruct((t, N_COL_BLOCKS * d), BF16),
        grid=(t // tm, N_COL_BLOCKS),
        in_specs=[
            pl.BlockSpec((tm, d), lambda i, j: (i, 0)),
            pl.BlockSpec((None, 1, d), mod_row(0)),
            pl.BlockSpec((None, 1, d), mod_row(1)),
            pl.BlockSpec((None, 1, d), lambda i, j: (layer, 0, 0)),
            pl.BlockSpec((None, d, d), lambda i, j: (layer, 0, j)),
            pl.BlockSpec((None, 1, d), lambda i, j: (layer, 0, 0)),
        ],
        out_specs=pl.BlockSpec((tm, d), lambda i, j: (i, j)),
        scratch_shapes=[pltpu.VMEM((tm, d), BF16)],
        compiler_params=_params(2),
        name="in_projection",
    )(x, mod3, mod3, g_mix3, w_in, g_sg3)


SCORE_BOUND_SLACK = 1.01
SCORE_BOUND_PAD = 1e-3
EXP_UNDERFLOW = 105.0
SAFE_SCORE_BOUND = 28.0
POS_SPLIT = 256


def _plan_kernel(q_ref, k_ref, sel_ref, slope_ref, r_ref, m_ref, kmax_ref, qn_ref, sf_ref, kn_ref, *, tq):
    t = pl.program_id(1)
    tp = q_ref.shape[0]
    q = q_ref[...].astype(F32)
    k = k_ref[...].astype(F32)

    def group_sums(x):
        hi = x.astype(BF16)
        lo = (x - hi.astype(F32)).astype(BF16)
        return _dot(hi, sel_ref[...]) + _dot(lo, sel_ref[...])

    rows = pl.ds(pl.multiple_of(t * tp, tp), tp)
    qn_ref[rows, :] = group_sums(q * q)
    sf_ref[rows, :] = group_sums(q * k)
    kn = jnp.max(group_sums(k * k), axis=0, keepdims=True)

    @pl.when(t == 0)
    def _():
        kn_ref[...] = jnp.zeros_like(kn_ref)

    kn_ref[...] = jnp.maximum(kn_ref[...], kn)

    @pl.when(t == pl.num_programs(1) - 1)
    def _():
        kmax = jnp.sqrt(kn_ref[...])
        kmax_ref[...] = jnp.broadcast_to(kmax, kmax_ref.shape)

        def tile(i, carry):
            tile_rows = pl.ds(pl.multiple_of(i * tq, tq), tq)
            m = jnp.sqrt(qn_ref[tile_rows, :]) * kmax * SCORE_BOUND_SLACK + SCORE_BOUND_PAD
            r = (m - sf_ref[tile_rows, :] + EXP_UNDERFLOW) / slope_ref[...]
            m_ref[pl.ds(i, 1), :] = jnp.max(m, axis=0, keepdims=True)
            r_ref[pl.ds(i, 1), :] = jnp.max(r, axis=0, keepdims=True)
            return carry

        lax.fori_loop(0, qn_ref.shape[0] // tq, tile, 0)


def _attention_plan(z, batch, seq, slopes, *, tq, tk):
    n_q, n_k = seq // tq, seq // tk
    tp = min(seq, 1024)
    col = jnp.arange(D_MODEL) // HEAD_DIM
    sel = (col[:, None] == jnp.arange(LANES)[None, :]).astype(BF16)
    slope_row = jnp.ones((LANES,), F32).at[:2 * N_HEADS].set(jnp.repeat(slopes, 2)).reshape(1, LANES)
    stat = jax.ShapeDtypeStruct((batch, n_q, LANES), F32)
    r, m, kmax = pl.pallas_call(
        functools.partial(_plan_kernel, tq=tq),
        out_shape=(stat, stat, jax.ShapeDtypeStruct((batch, 8, LANES), F32)),
        grid=(batch, seq // tp),
        in_specs=[
            pl.BlockSpec((tp, D_MODEL), lambda b, t: (b * (seq // tp) + t, COL_Q)),
            pl.BlockSpec((tp, D_MODEL), lambda b, t: (b * (seq // tp) + t, COL_K)),
            pl.BlockSpec((D_MODEL, LANES), lambda b, t: (0, 0)),
            pl.BlockSpec((1, LANES), lambda b, t: (0, 0)),
        ],
        out_specs=(pl.BlockSpec((None, n_q, LANES), lambda b, t: (b, 0, 0)),
                   pl.BlockSpec((None, n_q, LANES), lambda b, t: (b, 0, 0)),
                   pl.BlockSpec((None, 8, LANES), lambda b, t: (b, 0, 0))),
        scratch_shapes=[pltpu.VMEM((seq, LANES), F32), pltpu.VMEM((seq, LANES), F32),
                        pltpu.VMEM((1, LANES), F32)],
        compiler_params=_params(2),
        name="attention_plan",
    )(z, z, sel, slope_row)

    def per_head(a):
        return jnp.max(a[..., :2 * N_HEADS].reshape(batch, n_q, N_HEADS, 2), axis=-1).transpose(0, 2, 1)

    r, m = per_head(r), per_head(m)
    r = jnp.where(r == r, r, jnp.inf)
    q0 = (jnp.arange(n_q) * tq).astype(F32)[None, None, :]
    first_diag = (jnp.arange(n_q) * tq // tk)[None, None, :]
    last_diag = first_diag + (tq // tk - 1)
    lo = jnp.clip(jnp.ceil((q0 + 1.0 - r) / tk - 1.0), 0, n_k).astype(jnp.int32)
    hi = jnp.clip(jnp.floor((r + q0 + (tq - 1.0)) / tk), 0, n_k - 1).astype(jnp.int32)
    lo = jnp.minimum(lo, first_diag)
    hi = jnp.maximum(hi, last_diag)
    safe = (m <= SAFE_SCORE_BOUND).astype(jnp.int32)
    head_stat = jnp.zeros((batch, N_HEADS, 1, LANES), F32)
    head_stat = head_stat.at[:, :, 0, 0].set(jnp.broadcast_to(slopes, (batch, N_HEADS)))
    head_stat = head_stat.at[:, :, 0, 1].set(kmax[:, 0, 0:2 * N_HEADS:2])
    head_stat = head_stat.at[:, :, 0, 2].set(kmax[:, 0, 1:2 * N_HEADS:2])
    return lo.reshape(-1), hi.reshape(-1), safe.reshape(-1), head_stat


def _key_position_features(seq):
    pos = jnp.arange(seq)
    feat = jnp.zeros((seq, LANES), F32)
    feat = feat.at[:, 0].set((pos // POS_SPLIT * POS_SPLIT).astype(F32))
    feat = feat.at[:, 1].set((pos % POS_SPLIT).astype(F32))
    feat = feat.at[:, 2:5].set(1.0)
    return feat.astype(BF16)


def _attn_kernel(lo_ref, hi_ref, safe_ref, q_ref, k_ref, v_ref, kf_ref, absrel_ref, hs_ref,
                 lq1_ref, lk1_ref, lq2_ref, lk2_ref, gsub_ref,
                 o_ref, vt_ref, acc0_ref, acc1_ref, l0_ref, l1_ref, p_ref, dbias_ref, *, tk, lam_init):
    b, h, i = pl.program_id(0), pl.program_id(1), pl.program_id(2)
    plan = (b * N_HEADS + h) * pl.num_programs(2) + i
    tq = q_ref.shape[0]
    n_k = k_ref.shape[0] // tk

    head_stat = hs_ref[...]
    slope = head_stat[:, 0:1]

    @pl.when(i == 0)
    def _():
        for j in range(n_k):
            vt_ref[j] = v_ref[pl.ds(j * tk, tk), :].T
        dbias_ref[...] = slope * absrel_ref[...]

    q = q_ref[...]
    lane = lax.broadcasted_iota(jnp.int32, q.shape, 1)
    zero = jnp.zeros_like(q)
    qt0 = jnp.where(lane < HEAD_DIM, q, zero).T
    qt1 = jnp.where(lane >= HEAD_DIM, q, zero).T
    q_start = i * tq

    def key_rows(j):
        return pl.ds(pl.multiple_of(j * tk, tk), tk)

    acc0_ref[...] = jnp.zeros_like(acc0_ref)
    acc1_ref[...] = jnp.zeros_like(acc1_ref)
    l0_ref[...] = jnp.zeros_like(l0_ref)
    l1_ref[...] = jnp.zeros_like(l1_ref)

    @pl.when(safe_ref[plan] == 1)
    def _():
        column = lax.broadcasted_iota(jnp.int32, (1, tq), 1)
        qpos = q_start + column
        q_hi = (qpos // POS_SPLIT * POS_SPLIT).astype(F32)
        q_lo = (qpos % POS_SPLIT).astype(F32)
        row = lax.broadcasted_iota(jnp.int32, (HEAD_WIDTH, tq), 0)

        def query_features(sign, bound):
            feat = jnp.where(row < 2, sign * slope,
                             jnp.where(row == 2, -sign * slope * q_hi,
                                       jnp.where(row == 3, -sign * slope * q_lo, 0.0)))
            return jnp.where(row == 4, -bound, feat).astype(BF16)

        def score_bound(qt, kmax):
            qf = qt.astype(F32)
            norm = jnp.sqrt(jnp.sum(qf * qf, axis=0, keepdims=True))
            return norm * kmax * SCORE_BOUND_SLACK + SCORE_BOUND_PAD

        m0 = score_bound(qt0, head_stat[:, 1:2])
        m1 = score_bound(qt1, head_stat[:, 2:3])

        def operands(sign):
            return (jnp.concatenate([qt0, query_features(sign, m0)], axis=0),
                    jnp.concatenate([qt1, query_features(sign, m1)], axis=0))

        def probabilities(s, l_ref):
            p = jnp.exp(s)
            l_ref[...] += jnp.sum(p.reshape(tk // 8, 8, tq), axis=0)
            return p.astype(BF16)

        def tile_probabilities(j, qa0, qa1, mixed_block=None):
            ka = jnp.concatenate([k_ref[key_rows(j), :], kf_ref[key_rows(j), :]], axis=1)
            out = []
            for qa, l_ref in ((qa0, l0_ref), (qa1, l1_ref)):
                s = _dot(ka, qa)
                if mixed_block is not None:
                    parts = [s[:, c * tk:(c + 1) * tk] for c in range(tq // tk)]
                    parts[mixed_block] = parts[mixed_block] - dbias_ref[...]
                    s = jnp.concatenate(parts, axis=1) if len(parts) > 1 else parts[0]
                out.append(probabilities(s, l_ref))
            return out

        def accumulate(j):
            vt = vt_ref[j]
            acc0_ref[...] += _dot(vt, p_ref[0])
            acc1_ref[...] += _dot(vt, p_ref[1])

        def stash(p0, p1):
            p_ref[0] = p0
            p_ref[1] = p1

        def run(first, last, sign, pending):
            qa0, qa1 = operands(sign)

            def body(j, pending):
                accumulate(pending)
                stash(*tile_probabilities(j, qa0, qa1))
                return j

            return lax.fori_loop(first, last, body, pending)

        first_diag = q_start // tk
        for block in range(tq // tk):
            sign = jnp.sign((column // tk - block).astype(F32))
            if block > 0:
                accumulate(first_diag + block - 1)
            stash(*tile_probabilities(first_diag + block, *operands(sign), mixed_block=block))
        pending = run(lo_ref[plan], first_diag, 1.0, first_diag + tq // tk - 1)
        pending = run(first_diag + tq // tk, hi_ref[plan] + 1, -1.0, pending)
        accumulate(pending)

    @pl.when(safe_ref[plan] == 0)
    def _():
        rel = (lax.broadcasted_iota(jnp.int32, (tk, tq), 0)
               - lax.broadcasted_iota(jnp.int32, (tk, tq), 1)).astype(F32)

        def alibi(j):
            return slope * jnp.abs(rel + (j * tk - q_start).astype(F32))

        def update(s, m, l, acc_ref, vt):
            m_new = jnp.maximum(m, jnp.max(s, axis=0, keepdims=True))
            alpha = jnp.exp(m - m_new)
            p = jnp.exp(s - m_new)
            l_new = alpha * l + jnp.sum(p, axis=0, keepdims=True)
            acc_ref[...] = alpha * acc_ref[...] + _dot(vt, p.astype(BF16))
            return m_new, l_new

        def body(j, carry):
            m0, l0, m1, l1 = carry
            kt = k_ref[key_rows(j), :]
            bias = alibi(j)
            vt = vt_ref[j]
            m0, l0 = update(_dot(kt, qt0) - bias, m0, l0, acc0_ref, vt)
            m1, l1 = update(_dot(kt, qt1) - bias, m1, l1, acc1_ref, vt)
            return m0, l0, m1, l1

        init_m = jnp.full((1, tq), NEG_BIG, F32)
        init_l = jnp.zeros((1, tq), F32)
        _, l0, _, l1 = lax.fori_loop(0, n_k, body, (init_m, init_l, init_m, init_l))
        l0_ref[0:1, :] = l0
        l1_ref[0:1, :] = l1

    l0 = jnp.sum(l0_ref[...], axis=0, keepdims=True)
    l1 = jnp.sum(l1_ref[...], axis=0, keepdims=True)
    lam = (jnp.exp(jnp.sum(lq1_ref[...] * lk1_ref[...], axis=-1, keepdims=True))
           - jnp.exp(jnp.sum(lq2_ref[...] * lk2_ref[...], axis=-1, keepdims=True)) + lam_init)
    out_t = acc0_ref[...] / l0 - lam * (acc1_ref[...] / l1)
    out = _rms(out_t.T) * gsub_ref[...] * (1.0 - lam_init)
    o_ref[...] = out.astype(BF16)


def _attention(z, batch, seq, layer, slopes, lam_params, g_subln3, *, tq, tk):
    t = z.shape[0]
    n_q = seq // tq
    blocks_per_col = D_MODEL // HEAD_WIDTH
    lo, hi, safe, head_stat = _attention_plan(z, batch, seq, slopes, tq=tq, tk=tk)
    pos = jnp.arange(tk, dtype=F32)
    abs_rel = jnp.abs(pos[:, None] - pos[None, :])
    lam_spec = pl.BlockSpec((None, 1, HEAD_DIM), lambda b, h, i, *_: (layer, 0, 0))
    grid_spec = pltpu.PrefetchScalarGridSpec(
        num_scalar_prefetch=3,
        grid=(batch, N_HEADS, n_q),
        in_specs=[
            pl.BlockSpec((tq, HEAD_WIDTH), lambda b, h, i, *_: (b * n_q + i, COL_Q * blocks_per_col + h)),
            pl.BlockSpec((seq, HEAD_WIDTH), lambda b, h, i, *_: (b, COL_K * blocks_per_col + h)),
            pl.BlockSpec((seq, HEAD_WIDTH), lambda b, h, i, *_: (b, COL_VA * blocks_per_col + h)),
            pl.BlockSpec((seq, LANES), lambda b, h, i, *_: (0, 0)),
            pl.BlockSpec((tk, tk), lambda b, h, i, *_: (0, 0)),
            pl.BlockSpec((None, None, 1, LANES), lambda b, h, i, *_: (b, h, 0, 0)),
            lam_spec, lam_spec, lam_spec, lam_spec,
            pl.BlockSpec((None, 1, HEAD_WIDTH), lambda b, h, i, *_: (layer, 0, 0)),
        ],
        out_specs=pl.BlockSpec((tq, HEAD_WIDTH), lambda b, h, i, *_: (b * n_q + i, h)),
        scratch_shapes=[
            pltpu.VMEM((seq // tk, HEAD_WIDTH, tk), BF16),
            pltpu.VMEM((HEAD_WIDTH, tq), F32),
            pltpu.VMEM((HEAD_WIDTH, tq), F32),
            pltpu.VMEM((8, tq), F32),
            pltpu.VMEM((8, tq), F32),
            pltpu.VMEM((2, tk, tq), BF16),
            pltpu.VMEM((tk, tk), F32),
        ],
    )
    return pl.pallas_call(
        functools.partial(_attn_kernel, tk=tk, lam_init=_lambda_init(layer)),
        out_shape=jax.ShapeDtypeStruct((t, D_MODEL), BF16),
        grid_spec=grid_spec,
        compiler_params=_params(3),
        name="diff_attention",
    )(lo, hi, safe, z, z, z, _key_position_features(seq), abs_rel, head_stat, *lam_params, g_subln3)


def _route(scores_t, biased_t):
    def row(a, e):
        return a[e:e + 1, :]

    grp_scores = []
    for g in range(N_GROUPS):
        a, b, c, d = (row(biased_t, g * EXPERTS_PER_GROUP + k) for k in range(EXPERTS_PER_GROUP))
        hi_ab, lo_ab = jnp.maximum(a, b), jnp.minimum(a, b)
        hi_cd, lo_cd = jnp.maximum(c, d), jnp.minimum(c, d)
        top1 = jnp.maximum(hi_ab, hi_cd)
        top2 = jnp.maximum(jnp.minimum(hi_ab, hi_cd), jnp.maximum(lo_ab, lo_cd))
        grp_scores.append(top1 + top2)
    best = grp_scores[0]
    sel = jnp.zeros_like(best, dtype=jnp.int32)
    for g in range(1, N_GROUPS):
        better = grp_scores[g] > best
        best = jnp.where(better, grp_scores[g], best)
        sel = jnp.where(better, g, sel)

    def pick(a, k):
        out = row(a, k)
        for g in range(1, N_GROUPS):
            out = jnp.where(sel == g, row(a, g * EXPERTS_PER_GROUP + k), out)
        return out

    cand_b = [pick(biased_t, k) for k in range(EXPERTS_PER_GROUP)]
    cand_s = [pick(scores_t, k) for k in range(EXPERTS_PER_GROUP)]
    v0, k0, w0 = cand_b[0], jnp.zeros_like(sel), cand_s[0]
    for k in range(1, EXPERTS_PER_GROUP):
        better = cand_b[k] > v0
        v0 = jnp.where(better, cand_b[k], v0)
        k0 = jnp.where(better, k, k0)
        w0 = jnp.where(better, cand_s[k], w0)
    v1 = jnp.full_like(v0, -jnp.inf)
    k1 = jnp.zeros_like(sel)
    w1 = jnp.zeros_like(w0)
    for k in range(EXPERTS_PER_GROUP):
        better = (k0 != k) & (cand_b[k] > v1)
        v1 = jnp.where(better, cand_b[k], v1)
        k1 = jnp.where(better, k, k1)
        w1 = jnp.where(better, cand_s[k], w1)
    total = w0 + w1
    e0 = sel * EXPERTS_PER_GROUP + k0
    e1 = sel * EXPERTS_PER_GROUP + k1
    return e0, e1, w0 / total, w1 / total


def _post_kernel(u_ref, v_ref, ga_ref, gb_ref, o_ref, x_ref, gt_ref, shf_ref, scf_ref, gffn_ref,
                 ws_ref, bs_ref, wa_ref, wb_ref, wo_ref, wr_ref, br_ref,
                 xmid_ref, h2_ref, comb_ref, a_ref):
    tm = x_ref.shape[0]
    for n in range(tm // CHUNK):
        rows = pl.ds(n * CHUNK, CHUNK)
        for g in range(GM_GROUPS):
            cols = pl.ds(g * GROUP_WIDTH, GROUP_WIDTH)
            mixed = _dot(ws_ref[g], v_ref[rows, cols]) + bs_ref[g]
            a_ref[rows, cols] = (u_ref[rows, cols].astype(F32) * mixed).astype(BF16)
    ya = _dot(a_ref[...], wa_ref[...])
    yb = _dot(o_ref[...], wb_ref[...])
    merged = ga_ref[...].astype(F32) * ya + gb_ref[...].astype(F32) * yb
    x_mid = x_ref[...] + gt_ref[...] * _dot(merged.astype(BF16), wo_ref[...])
    xmid_ref[...] = x_mid
    h2 = _rms(x_mid) * gffn_ref[...]
    h2 = (h2 * (1.0 + scf_ref[...]) + shf_ref[...]).astype(BF16)
    h2_ref[...] = h2
    scores = jax.nn.sigmoid(_dot(h2, wr_ref[...]))
    biased = scores + br_ref[...]
    e0, e1, w0, w1 = _route(scores.T, biased.T)
    expert = lax.broadcasted_iota(jnp.int32, (LANES, tm), 0)
    comb_t = jnp.where(expert == e0, w0, 0.0) + jnp.where(expert == e1, w1, 0.0)
    comb_ref[...] = comb_t.T


def _post_attention(z, o, x, mod3, mod_base, seq, layer, g_ffn3, w_s, bs_b, w_br_a, w_br_b, w_out,
                    w_router_p, b_router_p, *, tm):
    t, d = x.shape
    tiles_per_seq = seq // tm

    def mod_row(which):
        return lambda i: (mod_base + (i // tiles_per_seq) * N_MOD + which, 0, 0)

    def z_col(col):
        return pl.BlockSpec((tm, d), lambda i: (i, col))

    def layer_mat():
        return pl.BlockSpec((None, d, d), lambda i: (layer, 0, 0))

    row_tile = pl.BlockSpec((tm, d), lambda i: (i, 0))
    return pl.pallas_call(
        _post_kernel,
        out_shape=(jax.ShapeDtypeStruct((t, d), F32),
                   jax.ShapeDtypeStruct((t, d), BF16),
                   jax.ShapeDtypeStruct((t, LANES), F32)),
        grid=(t // tm,),
        in_specs=[
            z_col(COL_U), z_col(COL_V), z_col(COL_GA), z_col(COL_GB),
            row_tile, row_tile,
            pl.BlockSpec((None, 1, d), mod_row(2)),
            pl.BlockSpec((None, 1, d), mod_row(3)),
            pl.BlockSpec((None, 1, d), mod_row(4)),
            pl.BlockSpec((None, 1, d), lambda i: (layer, 0, 0)),
            pl.BlockSpec((None, GM_GROUPS, CHUNK, CHUNK), lambda i: (layer, 0, 0, 0)),
            pl.BlockSpec((None, GM_GROUPS, CHUNK, GROUP_WIDTH), lambda i: (layer, 0, 0, 0)),
            layer_mat(), layer_mat(), layer_mat(),
            pl.BlockSpec((d, LANES), lambda i: (0, 0)),
            pl.BlockSpec((1, LANES), lambda i: (0, 0)),
        ],
        out_specs=(row_tile, row_tile, pl.BlockSpec((tm, LANES), lambda i: (i, 0))),
        scratch_shapes=[pltpu.VMEM((tm, d), BF16)],
        compiler_params=_params(1),
        name="post_attention",
    )(z, z, z, z, o, x, mod3, mod3, mod3, g_ffn3, w_s, bs_b, w_br_a, w_br_b, w_out,
      w_router_p, b_router_p)


def _moe_kernel(h_ref, comb_ref, x_ref, gt_ref, wg_ref, wu_ref, wd_ref, gfin_ref, out_ref, acc_ref,
                *, final_norm):
    e = pl.program_id(1)

    @pl.when(e == 0)
    def _():
        acc_ref[...] = jnp.zeros_like(acc_ref)

    h = h_ref[...]
    gate = _dot(h, wg_ref[...])
    hid = gate * jax.nn.sigmoid(gate) * _dot(h, wu_ref[...])
    comb = comb_ref[...]
    lane = lax.broadcasted_iota(jnp.int32, comb.shape, 1)
    weight = jnp.sum(jnp.where(lane == e, comb, 0.0), axis=-1, keepdims=True)
    acc_ref[...] += _dot((weight * hid).astype(BF16), wd_ref[...])

    @pl.when(e == N_EXPERTS - 1)
    def _():
        x = x_ref[...] + gt_ref[...] * acc_ref[...]
        if final_norm:
            x = _rms(x) * gfin_ref[...]
        out_ref[...] = x


def _moe(h2, comb, x_mid, mod3, mod_base, seq, layer, w_gate, w_up, w_down, g_final2, *, tm, final_norm):
    t, d = x_mid.shape
    tiles_per_seq = seq // tm
    row_tile = pl.BlockSpec((tm, d), lambda i, e: (i, 0))
    return pl.pallas_call(
        functools.partial(_moe_kernel, final_norm=final_norm),
        out_shape=jax.ShapeDtypeStruct((t, d), F32),
        grid=(t // tm, N_EXPERTS),
        in_specs=[
            row_tile,
            pl.BlockSpec((tm, LANES), lambda i, e: (i, 0)),
            row_tile,
            pl.BlockSpec((None, 1, d), lambda i, e: (mod_base + (i // tiles_per_seq) * N_MOD + 5, 0, 0)),
            pl.BlockSpec((None, None, d, D_EXPERT), lambda i, e: (layer, e, 0, 0)),
            pl.BlockSpec((None, None, d, D_EXPERT), lambda i, e: (layer, e, 0, 0)),
            pl.BlockSpec((None, None, D_EXPERT, d), lambda i, e: (layer, e, 0, 0)),
            pl.BlockSpec((1, d), lambda i, e: (0, 0)),
        ],
        out_specs=row_tile,
        scratch_shapes=[pltpu.VMEM((tm, d), F32)],
        compiler_params=_params(2),
        name="moe_experts",
    )(h2, comb, x_mid, mod3, w_gate, w_up, w_down, g_final2)


def _tile(seq, target):
    return min(seq, target)


def _trunk(x3, mod3, batch_offset, padded_batch, w, depth):
    batch, seq, d = x3.shape
    x = x3.reshape(batch * seq, d)
    for layer in range(depth):
        mod_base = (layer * padded_batch + batch_offset) * N_MOD
        z = _in_projection(x, mod3, mod_base, seq, layer, w["g_mix"], w["w_in"], w["g_sg"],
                           tm=_tile(seq, 1024))
        o = _attention(z, batch, seq, layer, w["slopes"], w["lam"], w["g_subln"],
                       tq=_tile(seq, 1024), tk=_tile(seq, 512))
        x_mid, h2, comb = _post_attention(z, o, x, mod3, mod_base, seq, layer, w["g_ffn"], w["w_s"],
                                          w["bs_b"], w["w_br_a"], w["w_br_b"], w["w_out"],
                                          w["w_router"], w["b_router"], tm=_tile(seq, 512))
        x = _moe(h2, comb, x_mid, mod3, mod_base, seq, layer, w["w_gate"], w["w_up"], w["w_down"],
                 w["g_final"], tm=_tile(seq, 1024), final_norm=(layer == depth - 1))
    return x.reshape(batch, seq, d)


def kernel(x_prompt, x_sample, c_prompt, c_sample, w_ada, b_ada, g_mix, w_in, g_sg, w_s, b_s, w_br_a,
           lam_q1, lam_k1, lam_q2, lam_k2, g_subln, w_br_b, w_out, g_ffn, w_router, b_router,
           w_gate, w_up, w_down, g_final):
    depth, d = g_mix.shape
    n_prompt, n_sample = c_prompt.shape[0], c_sample.shape[0]
    padded_batch = -(-(n_prompt + n_sample) // 8) * 8
    c_all = jnp.zeros((padded_batch, d), F32)
    c_all = c_all.at[:n_prompt].set(c_prompt).at[n_prompt:n_prompt + n_sample].set(c_sample)
    mod = _modulations(c_all, w_ada, b_ada)
    mod3 = mod.reshape(depth * padded_batch * N_MOD, 1, d)

    slopes = jnp.exp2(-(8.0 / N_HEADS) * jnp.arange(1, N_HEADS + 1, dtype=F32))
    w = {
        "g_mix": g_mix.reshape(depth, 1, d),
        "w_in": w_in.astype(BF16),
        "g_sg": g_sg.reshape(depth, 1, d),
        "slopes": slopes,
        "lam": tuple(p.reshape(depth, 1, HEAD_DIM) for p in (lam_q1, lam_k1, lam_q2, lam_k2)),
        "g_subln": g_subln.reshape(depth, 1, HEAD_WIDTH),
        "g_ffn": g_ffn.reshape(depth, 1, d),
        "w_s": w_s.astype(BF16),
        "bs_b": jnp.broadcast_to(b_s[..., None], b_s.shape + (GROUP_WIDTH,)),
        "w_br_a": w_br_a.astype(BF16),
        "w_br_b": w_br_b.astype(BF16),
        "w_out": w_out.astype(BF16),
        "w_router": jnp.pad(w_router, ((0, 0), (0, LANES - N_EXPERTS))).astype(BF16),
        "b_router": jnp.pad(b_router, (0, LANES - N_EXPERTS)).reshape(1, LANES),
        "w_gate": w_gate.astype(BF16),
        "w_up": w_up.astype(BF16),
        "w_down": w_down.astype(BF16),
        "g_final": g_final.reshape(1, d),
    }
    y_prompt = _trunk(x_prompt, mod3, 0, padded_batch, w, depth)
    y_sample = _trunk(x_sample, mod3, n_prompt, padded_batch, w, depth)
    return (y_prompt, y_sample)
```

```python
import functools
import math

import jax
import jax.numpy as jnp
from jax import lax
from jax.experimental import pallas as pl
from jax.experimental.pallas import tpu as pltpu

D_MODEL = 1024
N_HEADS = 8
HEAD_DIM = 64
HEAD_WIDTH = 2 * HEAD_DIM
CHUNK = 128
GM_GROUPS = 8
GROUP_WIDTH = D_MODEL // GM_GROUPS
N_EXPERTS = 16
N_GROUPS = 4
EXPERTS_PER_GROUP = N_EXPERTS // N_GROUPS
D_EXPERT = 512
N_MOD = 6
EPS = 1e-6
COL_U, COL_V, COL_Q, COL_K, COL_VA, COL_GA, COL_GB = range(7)
N_COL_BLOCKS = 7
LANES = 128
NEG_BIG = -1e30

F32 = jnp.float32
BF16 = jnp.bfloat16

VMEM_LIMIT = 56 * 1024 * 1024


def _params(n_axes):
    return pltpu.CompilerParams(dimension_semantics=("arbitrary",) * n_axes,
                                vmem_limit_bytes=VMEM_LIMIT)


def _lambda_init(layer):
    return 0.8 - 0.6 * math.exp(-0.3 * layer)


def _dot(a, b):
    return jnp.dot(a, b, preferred_element_type=F32)


def _mod_kernel(c_ref, w_ref, b_ref, o_ref):
    c = c_ref[...]
    a = c * jax.nn.sigmoid(c)
    o_ref[...] = _dot(a.astype(BF16), w_ref[...].astype(BF16)) + b_ref[...]


def _modulations(c_all, w_ada, b_ada):
    depth, d, n = w_ada.shape
    bp = c_all.shape[0]
    tn = n // 4
    return pl.pallas_call(
        _mod_kernel,
        out_shape=jax.ShapeDtypeStruct((depth, bp, n), F32),
        grid=(depth, n // tn),
        in_specs=[
            pl.BlockSpec((bp, d), lambda l, j: (0, 0)),
            pl.BlockSpec((None, d, tn), lambda l, j: (l, 0, j)),
            pl.BlockSpec((None, 1, tn), lambda l, j: (l, 0, j)),
        ],
        out_specs=pl.BlockSpec((None, bp, tn), lambda l, j: (l, 0, j)),
        compiler_params=_params(2),
        name="adaln_mod",
    )(c_all, w_ada, b_ada.reshape(depth, 1, n))


def _rms(x):
    return x * lax.rsqrt(jnp.mean(x * x, axis=-1, keepdims=True) + EPS)


def _in_kernel(x_ref, sh_ref, sc_ref, g_ref, w_ref, gsg_ref, z_ref, h_ref, *, rows):
    j = pl.program_id(1)
    tm = x_ref.shape[0]
    n_chunks = tm // rows

    @pl.when(j == 0)
    def _():
        for r in range(n_chunks):
            sl = pl.ds(r * rows, rows)
            h = _rms(x_ref[sl, :]) * g_ref[...]
            h = h * (1.0 + sc_ref[...]) + sh_ref[...]
            h_ref[sl, :] = h.astype(BF16)

    def project(epilogue):
        for r in range(n_chunks):
            sl = pl.ds(r * rows, rows)
            z_ref[sl, :] = epilogue(_dot(h_ref[sl, :], w_ref[...])).astype(BF16)

    @pl.when(j == COL_U)
    def _():
        project(jax.nn.gelu)

    @pl.when(j == COL_V)
    def _():
        project(lambda t: _rms(jax.nn.gelu(t)) * gsg_ref[...])

    @pl.when(j == COL_Q)
    def _():
        project(lambda t: t * (HEAD_DIM ** -0.5))

    @pl.when((j == COL_K) | (j == COL_VA))
    def _():
        project(lambda t: t)

    @pl.when((j == COL_GA) | (j == COL_GB))
    def _():
        project(jax.nn.sigmoid)


def _in_projection(x, mod3, mod_base, seq, layer, g_mix3, w_in, g_sg3, *, tm):
    t, d = x.shape
    tiles_per_seq = seq // tm

    def mod_row(which):
        return lambda i, j: (mod_base + (i // tiles_per_seq) * N_MOD + which, 0, 0)

    return pl.pallas_call(
        functools.partial(_in_kernel, rows=min(tm, 256)),
        out_shape=jax.ShapeDtypeStruct((t, N_COL_BLOCKS * d), BF16),
        grid=(t // tm, N_COL_BLOCKS),
        in_specs=[
            pl.BlockSpec((tm, d), lambda i, j: (i, 0)),
            pl.BlockSpec((None, 1, d), mod_row(0)),
            pl.BlockSpec((None, 1, d), mod_row(1)),
            pl.BlockSpec((None, 1, d), lambda i, j: (layer, 0, 0)),
            pl.BlockSpec((None, d, d), lambda i, j: (layer, 0, j)),
            pl.BlockSpec((None, 1, d), lambda i, j: (layer, 0, 0)),
        ],
        out_specs=pl.BlockSpec((tm, d), lambda i, j: (i, j)),
        scratch_shapes=[pltpu.VMEM((tm, d), BF16)],
        compiler_params=_params(2),
        name="in_projection",
    )(x, mod3, mod3, g_mix3, w_in, g_sg3)


SCORE_BOUND_SLACK = 1.01
SCORE_BOUND_PAD = 1e-3
EXP_UNDERFLOW = 105.0
SAFE_SCORE_BOUND = 28.0
POS_SPLIT = 256


def _plan_kernel(q_ref, k_ref, sel_ref, slope_ref, r_ref, m_ref, kmax_ref, qn_ref, sf_ref, kn_ref, *, tq):
    t = pl.program_id(1)
    tp = q_ref.shape[0]
    q = q_ref[...].astype(F32)
    k = k_ref[...].astype(F32)

    def group_sums(x):
        hi = x.astype(BF16)
        lo = (x - hi.astype(F32)).astype(BF16)
        return _dot(hi, sel_ref[...]) + _dot(lo, sel_ref[...])

    rows = pl.ds(pl.multiple_of(t * tp, tp), tp)
    qn_ref[rows, :] = group_sums(q * q)
    sf_ref[rows, :] = group_sums(q * k)
    kn = jnp.max(group_sums(k * k), axis=0, keepdims=True)

    @pl.when(t == 0)
    def _():
        kn_ref[...] = jnp.zeros_like(kn_ref)

    kn_ref[...] = jnp.maximum(kn_ref[...], kn)

    @pl.when(t == pl.num_programs(1) - 1)
    def _():
        kmax = jnp.sqrt(kn_ref[...])
        kmax_ref[...] = jnp.broadcast_to(kmax, kmax_ref.shape)

        def tile(i, carry):
            tile_rows = pl.ds(pl.multiple_of(i * tq, tq), tq)
            m = jnp.sqrt(qn_ref[tile_rows, :]) * kmax * SCORE_BOUND_SLACK + SCORE_BOUND_PAD
            r = (m - sf_ref[tile_rows, :] + EXP_UNDERFLOW) / slope_ref[...]
            m_ref[pl.ds(i, 1), :] = jnp.max(m, axis=0, keepdims=True)
            r_ref[pl.ds(i, 1), :] = jnp.max(r, axis=0, keepdims=True)
            return carry

        lax.fori_loop(0, qn_ref.shape[0] // tq, tile, 0)


def _attention_plan(z, batch, seq, slopes, *, tq, tk):
    n_q, n_k = seq // tq, seq // tk
    tp = min(seq, 1024)
    col = jnp.arange(D_MODEL) // HEAD_DIM
    sel = (col[:, None] == jnp.arange(LANES)[None, :]).astype(BF16)
    slope_row = jnp.ones((LANES,), F32).at[:2 * N_HEADS].set(jnp.repeat(slopes, 2)).reshape(1, LANES)
    stat = jax.ShapeDtypeStruct((batch, n_q, LANES), F32)
    r, m, kmax = pl.pallas_call(
        functools.partial(_plan_kernel, tq=tq),
        out_shape=(stat, stat, jax.ShapeDtypeStruct((batch, 8, LANES), F32)),
        grid=(batch, seq // tp),
        in_specs=[
            pl.BlockSpec((tp, D_MODEL), lambda b, t: (b * (seq // tp) + t, COL_Q)),
            pl.BlockSpec((tp, D_MODEL), lambda b, t: (b * (seq // tp) + t, COL_K)),
            pl.BlockSpec((D_MODEL, LANES), lambda b, t: (0, 0)),
            pl.BlockSpec((1, LANES), lambda b, t: (0, 0)),
        ],
        out_specs=(pl.BlockSpec((None, n_q, LANES), lambda b, t: (b, 0, 0)),
                   pl.BlockSpec((None, n_q, LANES), lambda b, t: (b, 0, 0)),
                   pl.BlockSpec((None, 8, LANES), lambda b, t: (b, 0, 0))),
        scratch_shapes=[pltpu.VMEM((seq, LANES), F32), pltpu.VMEM((seq, LANES), F32),
                        pltpu.VMEM((1, LANES), F32)],
        compiler_params=_params(2),
        name="attention_plan",
    )(z, z, sel, slope_row)

    def per_head(a):
        return jnp.max(a[..., :2 * N_HEADS].reshape(batch, n_q, N_HEADS, 2), axis=-1).transpose(0, 2, 1)

    r, m = per_head(r), per_head(m)
    r = jnp.where(r == r, r, jnp.inf)
    q0 = (jnp.arange(n_q) * tq).astype(F32)[None, None, :]
    first_diag = (jnp.arange(n_q) * tq // tk)[None, None, :]
    last_diag = first_diag + (tq // tk - 1)
    lo = jnp.clip(jnp.ceil((q0 + 1.0 - r) / tk - 1.0), 0, n_k).astype(jnp.int32)
    hi = jnp.clip(jnp.floor((r + q0 + (tq - 1.0)) / tk), 0, n_k - 1).astype(jnp.int32)
    lo = jnp.minimum(lo, first_diag)
    hi = jnp.maximum(hi, last_diag)
    safe = (m <= SAFE_SCORE_BOUND).astype(jnp.int32)
    head_stat = jnp.zeros((batch, N_HEADS, 1, LANES), F32)
    head_stat = head_stat.at[:, :, 0, 0].set(jnp.broadcast_to(slopes, (batch, N_HEADS)))
    head_stat = head_stat.at[:, :, 0, 1].set(kmax[:, 0, 0:2 * N_HEADS:2])
    head_stat = head_stat.at[:, :, 0, 2].set(kmax[:, 0, 1:2 * N_HEADS:2])
    return lo.reshape(-1), hi.reshape(-1), safe.reshape(-1), head_stat


def _key_position_features(seq):
    pos = jnp.arange(seq)
    feat = jnp.zeros((seq, LANES), F32)
    feat = feat.at[:, 0].set((pos // POS_SPLIT * POS_SPLIT).astype(F32))
    feat = feat.at[:, 1].set((pos % POS_SPLIT).astype(F32))
    feat = feat.at[:, 2:5].set(1.0)
    return feat.astype(BF16)


def _attn_kernel(lo_ref, hi_ref, safe_ref, q_ref, k_ref, v_ref, kf_ref, absrel_ref, hs_ref,
                 lq1_ref, lk1_ref, lq2_ref, lk2_ref, gsub_ref,
                 o_ref, vt_ref, acc0_ref, acc1_ref, l0_ref, l1_ref, p_ref, dbias_ref, *, tk, lam_init):
    b, h, i = pl.program_id(0), pl.program_id(1), pl.program_id(2)
    plan = (b * N_HEADS + h) * pl.num_programs(2) + i
    tq = q_ref.shape[0]
    n_k = k_ref.shape[0] // tk

    head_stat = hs_ref[...]
    slope = head_stat[:, 0:1]

    @pl.when(i == 0)
    def _():
        for j in range(n_k):
            vt_ref[j] = v_ref[pl.ds(j * tk, tk), :].T
        dbias_ref[...] = slope * absrel_ref[...]

    q = q_ref[...]
    lane = lax.broadcasted_iota(jnp.int32, q.shape, 1)
    zero = jnp.zeros_like(q)
    qt0 = jnp.where(lane < HEAD_DIM, q, zero).T
    qt1 = jnp.where(lane >= HEAD_DIM, q, zero).T
    q_start = i * tq

    def key_rows(j):
        return pl.ds(pl.multiple_of(j * tk, tk), tk)

    acc0_ref[...] = jnp.zeros_like(acc0_ref)
    acc1_ref[...] = jnp.zeros_like(acc1_ref)
    l0_ref[...] = jnp.zeros_like(l0_ref)
    l1_ref[...] = jnp.zeros_like(l1_ref)

    @pl.when(safe_ref[plan] == 1)
    def _():
        column = lax.broadcasted_iota(jnp.int32, (1, tq), 1)
        qpos = q_start + column
        q_hi = (qpos // POS_SPLIT * POS_SPLIT).astype(F32)
        q_lo = (qpos % POS_SPLIT).astype(F32)
        row = lax.broadcasted_iota(jnp.int32, (HEAD_WIDTH, tq), 0)

        def query_features(sign, bound):
            feat = jnp.where(row < 2, sign * slope,
                             jnp.where(row == 2, -sign * slope * q_hi,
                                       jnp.where(row == 3, -sign * slope * q_lo, 0.0)))
            return jnp.where(row == 4, -bound, feat).astype(BF16)

        def score_bound(qt, kmax):
            qf = qt.astype(F32)
            norm = jnp.sqrt(jnp.sum(qf * qf, axis=0, keepdims=True))
            return norm * kmax * SCORE_BOUND_SLACK + SCORE_BOUND_PAD

        m0 = score_bound(qt0, head_stat[:, 1:2])
        m1 = score_bound(qt1, head_stat[:, 2:3])

        def operands(sign):
            return (jnp.concatenate([qt0, query_features(sign, m0)], axis=0),
                    jnp.concatenate([qt1, query_features(sign, m1)], axis=0))

        def probabilities(s, l_ref):
            p = jnp.exp(s)
            l_ref[...] += jnp.sum(p.reshape(tk // 8, 8, tq), axis=0)
            return p.astype(BF16)

        def tile_probabilities(j, qa0, qa1, mixed_block=None):
            ka = jnp.concatenate([k_ref[key_rows(j), :], kf_ref[key_rows(j), :]], axis=1)
            out = []
            for qa, l_ref in ((qa0, l0_ref), (qa1, l1_ref)):
                s = _dot(ka, qa)
                if mixed_block is not None:
                    parts = [s[:, c * tk:(c + 1) * tk] for c in range(tq // tk)]
                    parts[mixed_block] = parts[mixed_block] - dbias_ref[...]
                    s = jnp.concatenate(parts, axis=1) if len(parts) > 1 else parts[0]
                out.append(probabilities(s, l_ref))
            return out

        def accumulate(j):
            vt = vt_ref[j]
            acc0_ref[...] += _dot(vt, p_ref[0])
            acc1_ref[...] += _dot(vt, p_ref[1])

        def stash(p0, p1):
            p_ref[0] = p0
            p_ref[1] = p1

        def run(first, last, sign, pending):
            qa0, qa1 = operands(sign)

            def body(j, pending):
                accumulate(pending)
                stash(*tile_probabilities(j, qa0, qa1))
                return j

            return lax.fori_loop(first, last, body, pending)

        first_diag = q_start // tk
        for block in range(tq // tk):
            sign = jnp.sign((column // tk - block).astype(F32))
            if block > 0:
                accumulate(first_diag + block - 1)
            stash(*tile_probabilities(first_diag + block, *operands(sign), mixed_block=block))
        pending = run(lo_ref[plan], first_diag, 1.0, first_diag + tq // tk - 1)
        pending = run(first_diag + tq // tk, hi_ref[plan] + 1, -1.0, pending)
        accumulate(pending)

    @pl.when(safe_ref[plan] == 0)
    def _():
        rel = (lax.broadcasted_iota(jnp.int32, (tk, tq), 0)
               - lax.broadcasted_iota(jnp.int32, (tk, tq), 1)).astype(F32)

        def alibi(j):
            return slope * jnp.abs(rel + (j * tk - q_start).astype(F32))

        def update(s, m, l, acc_ref, vt):
            m_new = jnp.maximum(m, jnp.max(s, axis=0, keepdims=True))
            alpha = jnp.exp(m - m_new)
            p = jnp.exp(s - m_new)
            l_new = alpha * l + jnp.sum(p, axis=0, keepdims=True)
            acc_ref[...] = alpha * acc_ref[...] + _dot(vt, p.astype(BF16))
            return m_new, l_new

        def body(j, carry):
            m0, l0, m1, l1 = carry
            kt = k_ref[key_rows(j), :]
            bias = alibi(j)
            vt = vt_ref[j]
            m0, l0 = update(_dot(kt, qt0) - bias, m0, l0, acc0_ref, vt)
            m1, l1 = update(_dot(kt, qt1) - bias, m1, l1, acc1_ref, vt)
            return m0, l0, m1, l1

        init_m = jnp.full((1, tq), NEG_BIG, F32)
        init_l = jnp.zeros((1, tq), F32)
        _, l0, _, l1 = lax.fori_loop(0, n_k, body, (init_m, init_l, init_m, init_l))
        l0_ref[0:1, :] = l0
        l1_ref[0:1, :] = l1

    l0 = jnp.sum(l0_ref[...], axis=0, keepdims=True)
    l1 = jnp.sum(l1_ref[...], axis=0, keepdims=True)
    lam = (jnp.exp(jnp.sum(lq1_ref[...] * lk1_ref[...], axis=-1, keepdims=True))
           - jnp.exp(jnp.sum(lq2_ref[...] * lk2_ref[...], axis=-1, keepdims=True)) + lam_init)
    out_t = acc0_ref[...] / l0 - lam * (acc1_ref[...] / l1)
    out = _rms(out_t.T) * gsub_ref[...] * (1.0 - lam_init)
    o_ref[...] = out.astype(BF16)


def _attention(z, batch, seq, layer, slopes, lam_params, g_subln3, *, tq, tk):
    t = z.shape[0]
    n_q = seq // tq
    blocks_per_col = D_MODEL // HEAD_WIDTH
    lo, hi, safe, head_stat = _attention_plan(z, batch, seq, slopes, tq=tq, tk=tk)
    pos = jnp.arange(tk, dtype=F32)
    abs_rel = jnp.abs(pos[:, None] - pos[None, :])
    lam_spec = pl.BlockSpec((None, 1, HEAD_DIM), lambda b, h, i, *_: (layer, 0, 0))
    grid_spec = pltpu.PrefetchScalarGridSpec(
        num_scalar_prefetch=3,
        grid=(batch, N_HEADS, n_q),
        in_specs=[
            pl.BlockSpec((tq, HEAD_WIDTH), lambda b, h, i, *_: (b * n_q + i, COL_Q * blocks_per_col + h)),
            pl.BlockSpec((seq, HEAD_WIDTH), lambda b, h, i, *_: (b, COL_K * blocks_per_col + h)),
            pl.BlockSpec((seq, HEAD_WIDTH), lambda b, h, i, *_: (b, COL_VA * blocks_per_col + h)),
            pl.BlockSpec((seq, LANES), lambda b, h, i, *_: (0, 0)),
            pl.BlockSpec((tk, tk), lambda b, h, i, *_: (0, 0)),
            pl.BlockSpec((None, None, 1, LANES), lambda b, h, i, *_: (b, h, 0, 0)),
            lam_spec, lam_spec, lam_spec, lam_spec,
            pl.BlockSpec((None, 1, HEAD_WIDTH), lambda b, h, i, *_: (layer, 0, 0)),
        ],
        out_specs=pl.BlockSpec((tq, HEAD_WIDTH), lambda b, h, i, *_: (b * n_q + i, h)),
        scratch_shapes=[
            pltpu.VMEM((seq // tk, HEAD_WIDTH, tk), BF16),
            pltpu.VMEM((HEAD_WIDTH, tq), F32),
            pltpu.VMEM((HEAD_WIDTH, tq), F32),
            pltpu.VMEM((8, tq), F32),
            pltpu.VMEM((8, tq), F32),
            pltpu.VMEM((2, tk, tq), BF16),
            pltpu.VMEM((tk, tk), F32),
        ],
    )
    return pl.pallas_call(
        functools.partial(_attn_kernel, tk=tk, lam_init=_lambda_init(layer)),
        out_shape=jax.ShapeDtypeStruct((t, D_MODEL), BF16),
        grid_spec=grid_spec,
        compiler_params=_params(3),
        name="diff_attention",
    )(lo, hi, safe, z, z, z, _key_position_features(seq), abs_rel, head_stat, *lam_params, g_subln3)


def _route(scores_t, biased_t):
    def row(a, e):
        return a[e:e + 1, :]

    grp_scores = []
    for g in range(N_GROUPS):
        a, b, c, d = (row(biased_t, g * EXPERTS_PER_GROUP + k) for k in range(EXPERTS_PER_GROUP))
        hi_ab, lo_ab = jnp.maximum(a, b), jnp.minimum(a, b)
        hi_cd, lo_cd = jnp.maximum(c, d), jnp.minimum(c, d)
        top1 = jnp.maximum(hi_ab, hi_cd)
        top2 = jnp.maximum(jnp.minimum(hi_ab, hi_cd), jnp.maximum(lo_ab, lo_cd))
        grp_scores.append(top1 + top2)
    best = grp_scores[0]
    sel = jnp.zeros_like(best, dtype=jnp.int32)
    for g in range(1, N_GROUPS):
        better = grp_scores[g] > best
        best = jnp.where(better, grp_scores[g], best)
        sel = jnp.where(better, g, sel)

    def pick(a, k):
        out = row(a, k)
        for g in range(1, N_GROUPS):
            out = jnp.where(sel == g, row(a, g * EXPERTS_PER_GROUP + k), out)
        return out

    cand_b = [pick(biased_t, k) for k in range(EXPERTS_PER_GROUP)]
    cand_s = [pick(scores_t, k) for k in range(EXPERTS_PER_GROUP)]
    v0, k0, w0 = cand_b[0], jnp.zeros_like(sel), cand_s[0]
    for k in range(1, EXPERTS_PER_GROUP):
        better = cand_b[k] > v0
        v0 = jnp.where(better, cand_b[k], v0)
        k0 = jnp.where(better, k, k0)
        w0 = jnp.where(better, cand_s[k], w0)
    v1 = jnp.full_like(v0, -jnp.inf)
    k1 = jnp.zeros_like(sel)
    w1 = jnp.zeros_like(w0)
    for k in range(EXPERTS_PER_GROUP):
        better = (k0 != k) & (cand_b[k] > v1)
        v1 = jnp.where(better, cand_b[k], v1)
        k1 = jnp.where(better, k, k1)
        w1 = jnp.where(better, cand_s[k], w1)
    total = w0 + w1
    e0 = sel * EXPERTS_PER_GROUP + k0
    e1 = sel * EXPERTS_PER_GROUP + k1
    return sel, e0, e1, w0 / total, w1 / total


def _combine_weights(h2, wr_ref, br_ref):
    rows = h2.shape[0]
    scores = jax.nn.sigmoid(_dot(h2, wr_ref[...]))
    biased = scores + br_ref[...]
    sel, e0, e1, w0, w1 = _route(scores.T, biased.T)
    expert = lax.broadcasted_iota(jnp.int32, (LANES, rows), 0)
    comb_t = jnp.where(expert == e0, w0, 0.0) + jnp.where(expert == e1, w1, 0.0)
    return sel, comb_t.T


def _to_token_rows(ref, value):
    for s in range(D_MODEL // LANES):
        ref[:, s, :] = value[:, s * LANES:(s + 1) * LANES]


def _from_token_rows(ref):
    return jnp.concatenate([ref[:, s, :] for s in range(D_MODEL // LANES)], axis=1)


def _post_kernel(u_ref, v_ref, ga_ref, gb_ref, o_ref, x_ref, gt_ref, shf_ref, scf_ref, gffn_ref,
                 ws_ref, bs_ref, wa_ref, wb_ref, wo_ref, wr_ref, br_ref, upper_ref,
                 xmid_ref, h2_ref, route_ref, count_ref, a_ref, base_ref):
    tm = x_ref.shape[0]
    for n in range(tm // CHUNK):
        rows = pl.ds(n * CHUNK, CHUNK)
        for g in range(GM_GROUPS):
            cols = pl.ds(g * GROUP_WIDTH, GROUP_WIDTH)
            mixed = _dot(ws_ref[g], v_ref[rows, cols]) + bs_ref[g]
            a_ref[rows, cols] = (u_ref[rows, cols].astype(F32) * mixed).astype(BF16)
    ya = _dot(a_ref[...], wa_ref[...])
    yb = _dot(o_ref[...], wb_ref[...])
    merged = ga_ref[...].astype(F32) * ya + gb_ref[...].astype(F32) * yb
    x_mid = x_ref[...] + gt_ref[...] * _dot(merged.astype(BF16), wo_ref[...])
    xmid_ref[...] = x_mid
    h2 = _rms(x_mid) * gffn_ref[...]
    h2 = (h2 * (1.0 + scf_ref[...]) + shf_ref[...]).astype(BF16)
    _to_token_rows(h2_ref, h2.astype(F32))

    sel, _ = _combine_weights(h2, wr_ref, br_ref)

    @pl.when(pl.program_id(0) == 0)
    def _():
        base_ref[...] = jnp.zeros_like(base_ref)

    group = lax.broadcasted_iota(jnp.int32, (8, tm), 0)
    member = group == sel
    earlier = _dot(member.astype(BF16), upper_ref[...])
    base = base_ref[...]
    rank = jnp.sum(jnp.where(member, earlier + base[:, 0:1], 0.0), axis=0, keepdims=True)
    base = base + jnp.sum(member.astype(F32), axis=1, keepdims=True)
    base_ref[...] = base
    count_ref[...] = base
    route_ref[...] = jnp.where(group == 0, sel.astype(F32), jnp.where(group == 1, rank, 0.0))


def _post_attention(z, o, x, mod3, mod_base, seq, layer, g_ffn3, w_s, bs_b, w_br_a, w_br_b, w_out,
                    w_router_p, b_router_p, *, tm):
    t, d = x.shape
    tiles_per_seq = seq // tm

    def mod_row(which):
        return lambda i: (mod_base + (i // tiles_per_seq) * N_MOD + which, 0, 0)

    def z_col(col):
        return pl.BlockSpec((tm, d), lambda i: (i, col))

    def layer_mat():
        return pl.BlockSpec((None, d, d), lambda i: (layer, 0, 0))

    token = jnp.arange(tm)
    upper = (token[:, None] < token[None, :]).astype(BF16)
    row_tile = pl.BlockSpec((tm, d), lambda i: (i, 0))
    return pl.pallas_call(
        _post_kernel,
        out_shape=(jax.ShapeDtypeStruct((t, d), F32),
                   jax.ShapeDtypeStruct((t, d // LANES, LANES), F32),
                   jax.ShapeDtypeStruct((t // tm, 8, tm), F32),
                   jax.ShapeDtypeStruct((8, LANES), F32)),
        grid=(t // tm,),
        in_specs=[
            z_col(COL_U), z_col(COL_V), z_col(COL_GA), z_col(COL_GB),
            row_tile, row_tile,
            pl.BlockSpec((None, 1, d), mod_row(2)),
            pl.BlockSpec((None, 1, d), mod_row(3)),
            pl.BlockSpec((None, 1, d), mod_row(4)),
            pl.BlockSpec((None, 1, d), lambda i: (layer, 0, 0)),
            pl.BlockSpec((None, GM_GROUPS, CHUNK, CHUNK), lambda i: (layer, 0, 0, 0)),
            pl.BlockSpec((None, GM_GROUPS, CHUNK, GROUP_WIDTH), lambda i: (layer, 0, 0, 0)),
            layer_mat(), layer_mat(), layer_mat(),
            pl.BlockSpec((d, LANES), lambda i: (0, 0)),
            pl.BlockSpec((1, LANES), lambda i: (0, 0)),
            pl.BlockSpec((tm, tm), lambda i: (0, 0)),
        ],
        out_specs=(row_tile,
                   pl.BlockSpec((tm, d // LANES, LANES), lambda i: (i, 0, 0)),
                   pl.BlockSpec((None, 8, tm), lambda i: (i, 0, 0)),
                   pl.BlockSpec((8, LANES), lambda i: (0, 0))),
        scratch_shapes=[pltpu.VMEM((tm, d), BF16), pltpu.VMEM((8, LANES), F32)],
        compiler_params=_params(1),
        name="post_attention",
    )(z, z, z, z, o, x, mod3, mod3, mod3, g_ffn3, w_s, bs_b, w_br_a, w_br_b, w_out,
      w_router_p, b_router_p, upper)


MOE_ROWS = 256


def _dispatch_tables(route, counts, n_tokens):
    group = route[:, 0, :].reshape(-1).astype(jnp.int32)
    rank = route[:, 1, :].reshape(-1).astype(jnp.int32)
    counts = counts[:N_GROUPS, 0].astype(jnp.int32)
    padded = (counts + MOE_ROWS - 1) // MOE_ROWS * MOE_ROWS
    ends = jnp.cumsum(padded)
    n_tiles = n_tokens // MOE_ROWS + N_GROUPS
    n_rows = n_tiles * MOE_ROWS
    position = (ends - padded)[group] + rank
    token = jnp.arange(n_tokens, dtype=jnp.int32)
    source = jnp.zeros((n_rows,), jnp.int32).at[position].set(token)
    unused = jnp.ones((n_rows,), jnp.int32).at[position].set(0)
    target = (n_tokens + jnp.cumsum(unused) - 1).astype(jnp.int32).at[position].set(token)
    tile_group = jnp.minimum(jnp.searchsorted(ends, jnp.arange(n_tiles) * MOE_ROWS, side="right"),
                             N_GROUPS - 1).astype(jnp.int32)
    return (tile_group, source.reshape(n_tiles, 1, MOE_ROWS), target.reshape(n_tiles, 1, MOE_ROWS))


def _moe_kernel(tile_group_ref, src_ref, src_next_ref, dst_ref, h_hbm, wr_ref, br_ref,
                wg_ref, wu_ref, wd_ref, y_hbm, x_buf, y_buf, gather_sem, scatter_sem):
    g = pl.program_id(0)
    n_tiles = pl.num_programs(0)
    slot = g % 2
    rows = x_buf.shape[1]

    def start_gather(index_ref, to_slot):
        def body(r, carry):
            pltpu.make_async_copy(h_hbm.at[index_ref[0, r]], x_buf.at[to_slot, r],
                                  gather_sem.at[to_slot]).start()
            return carry
        lax.fori_loop(0, rows, body, 0)

    def wait_gather(at_slot):
        pltpu.make_async_copy(h_hbm.at[pl.ds(0, rows)], x_buf.at[at_slot], gather_sem.at[at_slot]).wait()

    def wait_scatter(at_slot):
        pltpu.make_async_copy(y_buf.at[at_slot], y_hbm.at[pl.ds(0, rows)], scatter_sem.at[at_slot]).wait()

    @pl.when(g == 0)
    def _():
        start_gather(src_ref, 0)

    @pl.when(g + 1 < n_tiles)
    def _():
        start_gather(src_next_ref, 1 - slot)

    wait_gather(slot)
    h = _from_token_rows(x_buf.at[slot]).astype(BF16)
    _, comb = _combine_weights(h, wr_ref, br_ref)
    lane = lax.broadcasted_iota(jnp.int32, comb.shape, 1)
    first_expert = tile_group_ref[g] * EXPERTS_PER_GROUP
    y = jnp.zeros((rows, D_MODEL), F32)
    for k in range(EXPERTS_PER_GROUP):
        gate = _dot(h, wg_ref[k])
        hid = gate * jax.nn.sigmoid(gate) * _dot(h, wu_ref[k])
        weight = jnp.sum(jnp.where(lane == first_expert + k, comb, 0.0), axis=-1, keepdims=True)
        y = y + _dot((weight * hid).astype(BF16), wd_ref[k])

    @pl.when(g >= 2)
    def _():
        wait_scatter(slot)

    _to_token_rows(y_buf.at[slot], y)

    def scatter(r, carry):
        pltpu.make_async_copy(y_buf.at[slot, r], y_hbm.at[dst_ref[0, r]], scatter_sem.at[slot]).start()
        return carry
    lax.fori_loop(0, rows, scatter, 0)

    @pl.when(g == n_tiles - 1)
    def _():
        wait_scatter(slot)

        @pl.when(n_tiles >= 2)
        def _():
            wait_scatter(1 - slot)


def _moe(h2_rows, route, counts, layer, w_router_p, b_router_p, w_gate, w_up, w_down):
    n_tokens, sub, _ = h2_rows.shape
    d = sub * LANES
    tile_group, source, target = _dispatch_tables(route, counts, n_tokens)
    n_tiles = source.shape[0]
    index_block = lambda offset: pl.BlockSpec(
        (None, 1, MOE_ROWS), lambda g, tg: (jnp.minimum(g + offset, n_tiles - 1), 0, 0),
        memory_space=pltpu.SMEM)
    expert_block = lambda a, b: pl.BlockSpec((None, EXPERTS_PER_GROUP, a, b), lambda g, tg: (layer, tg[g], 0, 0))
    grid_spec = pltpu.PrefetchScalarGridSpec(
        num_scalar_prefetch=1,
        grid=(n_tiles,),
        in_specs=[
            index_block(0), index_block(1), index_block(0),
            pl.BlockSpec(memory_space=pl.ANY),
            pl.BlockSpec((d, LANES), lambda g, tg: (0, 0)),
            pl.BlockSpec((1, LANES), lambda g, tg: (0, 0)),
            expert_block(d, D_EXPERT), expert_block(d, D_EXPERT), expert_block(D_EXPERT, d),
        ],
        out_specs=pl.BlockSpec(memory_space=pl.ANY),
        scratch_shapes=[
            pltpu.VMEM((2, MOE_ROWS, sub, LANES), F32),
            pltpu.VMEM((2, MOE_ROWS, sub, LANES), F32),
            pltpu.SemaphoreType.DMA((2,)),
            pltpu.SemaphoreType.DMA((2,)),
        ],
    )
    return pl.pallas_call(
        _moe_kernel,
        out_shape=jax.ShapeDtypeStruct((n_tiles * MOE_ROWS, sub, LANES), F32),
        grid_spec=grid_spec,
        compiler_params=_params(1),
        name="moe_experts",
    )(tile_group, source, source, target, h2_rows, w_router_p, b_router_p, w_gate, w_up, w_down)


def _residual_kernel(x_ref, y_ref, gt_ref, gfin_ref, out_ref, *, final_norm):
    x = x_ref[...] + gt_ref[...] * _from_token_rows(y_ref)
    if final_norm:
        x = _rms(x) * gfin_ref[...]
    out_ref[...] = x


def _moe_residual(x_mid, y_rows, mod3, mod_base, seq, g_final2, *, tm, final_norm):
    t, d = x_mid.shape
    tiles_per_seq = seq // tm
    row_tile = pl.BlockSpec((tm, d), lambda i: (i, 0))
    return pl.pallas_call(
        functools.partial(_residual_kernel, final_norm=final_norm),
        out_shape=jax.ShapeDtypeStruct((t, d), F32),
        grid=(t // tm,),
        in_specs=[
            row_tile,
            pl.BlockSpec((tm, d // LANES, LANES), lambda i: (i, 0, 0)),
            pl.BlockSpec((None, 1, d), lambda i: (mod_base + (i // tiles_per_seq) * N_MOD + 5, 0, 0)),
            pl.BlockSpec((1, d), lambda i: (0, 0)),
        ],
        out_specs=row_tile,
        compiler_params=_params(1),
        name="moe_residual",
    )(x_mid, y_rows, mod3, g_final2)


def _tile(seq, target):
    return min(seq, target)


def _trunk(x3, mod3, batch_offset, padded_batch, w, depth):
    batch, seq, d = x3.shape
    x = x3.reshape(batch * seq, d)
    for layer in range(depth):
        mod_base = (layer * padded_batch + batch_offset) * N_MOD
        z = _in_projection(x, mod3, mod_base, seq, layer, w["g_mix"], w["w_in"], w["g_sg"],
                           tm=_tile(seq, 1024))
        o = _attention(z, batch, seq, layer, w["slopes"], w["lam"], w["g_subln"],
                       tq=_tile(seq, 1024), tk=_tile(seq, 512))
        x_mid, h2_rows, route, counts = _post_attention(
            z, o, x, mod3, mod_base, seq, layer, w["g_ffn"], w["w_s"], w["bs_b"], w["w_br_a"],
            w["w_br_b"], w["w_out"], w["w_router"], w["b_router"], tm=_tile(seq, 512))
        y_rows = _moe(h2_rows, route, counts, layer, w["w_router"], w["b_router"],
                      w["w_gate"], w["w_up"], w["w_down"])
        x = _moe_residual(x_mid, y_rows, mod3, mod_base, seq, w["g_final"], tm=_tile(seq, 512),
                          final_norm=(layer == depth - 1))
    return x.reshape(batch, seq, d)


def kernel(x_prompt, x_sample, c_prompt, c_sample, w_ada, b_ada, g_mix, w_in, g_sg, w_s, b_s, w_br_a,
           lam_q1, lam_k1, lam_q2, lam_k2, g_subln, w_br_b, w_out, g_ffn, w_router, b_router,
           w_gate, w_up, w_down, g_final):
    depth, d = g_mix.shape
    n_prompt, n_sample = c_prompt.shape[0], c_sample.shape[0]
    padded_batch = -(-(n_prompt + n_sample) // 8) * 8
    c_all = jnp.zeros((padded_batch, d), F32)
    c_all = c_all.at[:n_prompt].set(c_prompt).at[n_prompt:n_prompt + n_sample].set(c_sample)
    mod = _modulations(c_all, w_ada, b_ada)
    mod3 = mod.reshape(depth * padded_batch * N_MOD, 1, d)

    slopes = jnp.exp2(-(8.0 / N_HEADS) * jnp.arange(1, N_HEADS + 1, dtype=F32))
    w = {
        "g_mix": g_mix.reshape(depth, 1, d),
        "w_in": w_in.astype(BF16),
        "g_sg": g_sg.reshape(depth, 1, d),
        "slopes": slopes,
        "lam": tuple(p.reshape(depth, 1, HEAD_DIM) for p in (lam_q1, lam_k1, lam_q2, lam_k2)),
        "g_subln": g_subln.reshape(depth, 1, HEAD_WIDTH),
        "g_ffn": g_ffn.reshape(depth, 1, d),
        "w_s": w_s.astype(BF16),
        "bs_b": jnp.broadcast_to(b_s[..., None], b_s.shape + (GROUP_WIDTH,)),
        "w_br_a": w_br_a.astype(BF16),
        "w_br_b": w_br_b.astype(BF16),
        "w_out": w_out.astype(BF16),
        "w_router": jnp.pad(w_router, ((0, 0), (0, LANES - N_EXPERTS))).astype(BF16),
        "b_router": jnp.pad(b_router, (0, LANES - N_EXPERTS)).reshape(1, LANES),
        "w_gate": w_gate.astype(BF16),
        "w_up": w_up.astype(BF16),
        "w_down": w_down.astype(BF16),
        "g_final": g_final.reshape(1, d),
    }
    y_prompt = _trunk(x_prompt, mod3, 0, padded_batch, w, depth)
    y_sample = _trunk(x_sample, mod3, n_prompt, padded_batch, w, depth)
    return (y_prompt, y_sample)
```

```python
import functools
import math

import jax
import jax.numpy as jnp
from jax import lax
from jax.experimental import pallas as pl
from jax.experimental.pallas import tpu as pltpu

D_MODEL = 1024
N_HEADS = 8
HEAD_DIM = 64
HEAD_WIDTH = 2 * HEAD_DIM
CHUNK = 128
GM_GROUPS = 8
GROUP_WIDTH = D_MODEL // GM_GROUPS
N_EXPERTS = 16
N_GROUPS = 4
EXPERTS_PER_GROUP = N_EXPERTS // N_GROUPS
D_EXPERT = 512
N_MOD = 6
EPS = 1e-6
COL_U, COL_V, COL_Q, COL_K, COL_VA, COL_GA, COL_GB = range(7)
N_COL_BLOCKS = 7
LANES = 128
NEG_BIG = -1e30

F32 = jnp.float32
BF16 = jnp.bfloat16

VMEM_LIMIT = 56 * 1024 * 1024


def _params(n_axes):
    return pltpu.CompilerParams(dimension_semantics=("arbitrary",) * n_axes,
                                vmem_limit_bytes=VMEM_LIMIT)


def _lambda_init(layer):
    return 0.8 - 0.6 * math.exp(-0.3 * layer)


def _dot(a, b):
    return jnp.dot(a, b, preferred_element_type=F32)


def _mod_kernel(c_ref, w_ref, b_ref, o_ref):
    c = c_ref[...]
    a = c * jax.nn.sigmoid(c)
    o_ref[...] = _dot(a.astype(BF16), w_ref[...].astype(BF16)) + b_ref[...]


def _modulations(c_all, w_ada, b_ada):
    depth, d, n = w_ada.shape
    bp = c_all.shape[0]
    tn = n // 4
    return pl.pallas_call(
        _mod_kernel,
        out_shape=jax.ShapeDtypeStruct((depth, bp, n), F32),
        grid=(depth, n // tn),
        in_specs=[
            pl.BlockSpec((bp, d), lambda l, j: (0, 0)),
            pl.BlockSpec((None, d, tn), lambda l, j: (l, 0, j)),
            pl.BlockSpec((None, 1, tn), lambda l, j: (l, 0, j)),
        ],
        out_specs=pl.BlockSpec((None, bp, tn), lambda l, j: (l, 0, j)),
        compiler_params=_params(2),
        name="adaln_mod",
    )(c_all, w_ada, b_ada.reshape(depth, 1, n))


def _rms(x):
    return x * lax.rsqrt(jnp.mean(x * x, axis=-1, keepdims=True) + EPS)


def _in_kernel(x_ref, sh_ref, sc_ref, g_ref, w_ref, gsg_ref, z_ref, h_ref, *, rows):
    j = pl.program_id(1)
    tm = x_ref.shape[0]
    n_chunks = tm // rows

    @pl.when(j == 0)
    def _():
        for r in range(n_chunks):
            sl = pl.ds(r * rows, rows)
            h = _rms(x_ref[sl, :]) * g_ref[...]
            h = h * (1.0 + sc_ref[...]) + sh_ref[...]
            h_ref[sl, :] = h.astype(BF16)

    def project(epilogue):
        for r in range(n_chunks):
            sl = pl.ds(r * rows, rows)
            z_ref[sl, :] = epilogue(_dot(h_ref[sl, :], w_ref[...])).astype(BF16)

    @pl.when(j == COL_U)
    def _():
        project(jax.nn.gelu)

    @pl.when(j == COL_V)
    def _():
        project(lambda t: _rms(jax.nn.gelu(t)) * gsg_ref[...])

    @pl.when(j == COL_Q)
    def _():
        project(lambda t: t * (HEAD_DIM ** -0.5))

    @pl.when((j == COL_K) | (j == COL_VA))
    def _():
        project(lambda t: t)

    @pl.when((j == COL_GA) | (j == COL_GB))
    def _():
        project(jax.nn.sigmoid)


def _in_projection(x, mod3, mod_base, seq, layer, g_mix3, w_in, g_sg3, *, tm):
    t, d = x.shape
    tiles_per_seq = seq // tm

    def mod_row(which):
        return lambda i, j: (mod_base + (i // tiles_per_seq) * N_MOD + which, 0, 0)

    return pl.pallas_call(
        functools.partial(_in_kernel, rows=min(tm, 256)),
        out_shape=jax.ShapeDtypeStruct((t, N_COL_BLOCKS * d), BF16),
        grid=(t // tm, N_COL_BLOCKS),
        in_specs=[
            pl.BlockSpec((tm, d), lambda i, j: (i, 0)),
            pl.BlockSpec((None, 1, d), mod_row(0)),
            pl.BlockSpec((None, 1, d), mod_row(1)),
            pl.BlockSpec((None, 1, d), lambda i, j: (layer, 0, 0)),
            pl.BlockSpec((None, d, d), lambda i, j: (layer, 0, j)),
            pl.BlockSpec((None, 1, d), lambda i, j: (layer, 0, 0)),
        ],
        out_specs=pl.BlockSpec((tm, d), lambda i, j: (i, j)),
        scratch_shapes=[pltpu.VMEM((tm, d), BF16)],
        compiler_params=_params(2),
        name="in_projection",
    )(x, mod3, mod3, g_mix3, w_in, g_sg3)


SCORE_BOUND_SLACK = 1.01
SCORE_BOUND_PAD = 1e-3
EXP_UNDERFLOW = 105.0
SAFE_SCORE_BOUND = 28.0
POS_SPLIT = 256


def _plan_kernel(q_ref, k_ref, sel_ref, slope_ref, r_ref, m_ref, kmax_ref, qn_ref, sf_ref, kn_ref, *, tq):
    t = pl.program_id(1)
    tp = q_ref.shape[0]
    q = q_ref[...].astype(F32)
    k = k_ref[...].astype(F32)

    def group_sums(x):
        hi = x.astype(BF16)
        lo = (x - hi.astype(F32)).astype(BF16)
        return _dot(hi, sel_ref[...]) + _dot(lo, sel_ref[...])

    rows = pl.ds(pl.multiple_of(t * tp, tp), tp)
    qn_ref[rows, :] = group_sums(q * q)
    sf_ref[rows, :] = group_sums(q * k)
    kn = jnp.max(group_sums(k * k), axis=0, keepdims=True)

    @pl.when(t == 0)
    def _():
        kn_ref[...] = jnp.zeros_like(kn_ref)

    kn_ref[...] = jnp.maximum(kn_ref[...], kn)

    @pl.when(t == pl.num_programs(1) - 1)
    def _():
        kmax = jnp.sqrt(kn_ref[...])
        kmax_ref[...] = jnp.broadcast_to(kmax, kmax_ref.shape)

        def tile(i, carry):
            tile_rows = pl.ds(pl.multiple_of(i * tq, tq), tq)
            m = jnp.sqrt(qn_ref[tile_rows, :]) * kmax * SCORE_BOUND_SLACK + SCORE_BOUND_PAD
            r = (m - sf_ref[tile_rows, :] + EXP_UNDERFLOW) / slope_ref[...]
            m_ref[pl.ds(i, 1), :] = jnp.max(m, axis=0, keepdims=True)
            r_ref[pl.ds(i, 1), :] = jnp.max(r, axis=0, keepdims=True)
            return carry

        lax.fori_loop(0, qn_ref.shape[0] // tq, tile, 0)


def _attention_plan(z, batch, seq, slopes, *, tq, tk):
    n_q, n_k = seq // tq, seq // tk
    tp = min(seq, 1024)
    col = jnp.arange(D_MODEL) // HEAD_DIM
    sel = (col[:, None] == jnp.arange(LANES)[None, :]).astype(BF16)
    slope_row = jnp.ones((LANES,), F32).at[:2 * N_HEADS].set(jnp.repeat(slopes, 2)).reshape(1, LANES)
    stat = jax.ShapeDtypeStruct((batch, n_q, LANES), F32)
    r, m, kmax = pl.pallas_call(
        functools.partial(_plan_kernel, tq=tq),
        out_shape=(stat, stat, jax.ShapeDtypeStruct((batch, 8, LANES), F32)),
        grid=(batch, seq // tp),
        in_specs=[
            pl.BlockSpec((tp, D_MODEL), lambda b, t: (b * (seq // tp) + t, COL_Q)),
            pl.BlockSpec((tp, D_MODEL), lambda b, t: (b * (seq // tp) + t, COL_K)),
            pl.BlockSpec((D_MODEL, LANES), lambda b, t: (0, 0)),
            pl.BlockSpec((1, LANES), lambda b, t: (0, 0)),
        ],
        out_specs=(pl.BlockSpec((None, n_q, LANES), lambda b, t: (b, 0, 0)),
                   pl.BlockSpec((None, n_q, LANES), lambda b, t: (b, 0, 0)),
                   pl.BlockSpec((None, 8, LANES), lambda b, t: (b, 0, 0))),
        scratch_shapes=[pltpu.VMEM((seq, LANES), F32), pltpu.VMEM((seq, LANES), F32),
                        pltpu.VMEM((1, LANES), F32)],
        compiler_params=_params(2),
        name="attention_plan",
    )(z, z, sel, slope_row)

    def per_head(a):
        return jnp.max(a[..., :2 * N_HEADS].reshape(batch, n_q, N_HEADS, 2), axis=-1).transpose(0, 2, 1)

    r, m = per_head(r), per_head(m)
    r = jnp.where(r == r, r, jnp.inf)
    q0 = (jnp.arange(n_q) * tq).astype(F32)[None, None, :]
    first_diag = (jnp.arange(n_q) * tq // tk)[None, None, :]
    last_diag = first_diag + (tq // tk - 1)
    lo = jnp.clip(jnp.ceil((q0 + 1.0 - r) / tk - 1.0), 0, n_k).astype(jnp.int32)
    hi = jnp.clip(jnp.floor((r + q0 + (tq - 1.0)) / tk), 0, n_k - 1).astype(jnp.int32)
    lo = jnp.minimum(lo, first_diag)
    hi = jnp.maximum(hi, last_diag)
    safe = (m <= SAFE_SCORE_BOUND).astype(jnp.int32)
    head_stat = jnp.zeros((batch, N_HEADS, 1, LANES), F32)
    head_stat = head_stat.at[:, :, 0, 0].set(jnp.broadcast_to(slopes, (batch, N_HEADS)))
    head_stat = head_stat.at[:, :, 0, 1].set(kmax[:, 0, 0:2 * N_HEADS:2])
    head_stat = head_stat.at[:, :, 0, 2].set(kmax[:, 0, 1:2 * N_HEADS:2])
    return lo.reshape(-1), hi.reshape(-1), safe.reshape(-1), head_stat


def _key_position_features(seq):
    pos = jnp.arange(seq)
    feat = jnp.zeros((seq, LANES), F32)
    feat = feat.at[:, 0].set((pos // POS_SPLIT * POS_SPLIT).astype(F32))
    feat = feat.at[:, 1].set((pos % POS_SPLIT).astype(F32))
    feat = feat.at[:, 2:5].set(1.0)
    return feat.astype(BF16)


def _attn_kernel(lo_ref, hi_ref, safe_ref, q_ref, k_ref, v_ref, kf_ref, absrel_ref, hs_ref,
                 lq1_ref, lk1_ref, lq2_ref, lk2_ref, gsub_ref,
                 o_ref, vt_ref, acc0_ref, acc1_ref, l0_ref, l1_ref, p_ref, dbias_ref, *, tk, lam_init):
    b, h, i = pl.program_id(0), pl.program_id(1), pl.program_id(2)
    plan = (b * N_HEADS + h) * pl.num_programs(2) + i
    tq = q_ref.shape[0]
    n_k = k_ref.shape[0] // tk

    head_stat = hs_ref[...]
    slope = head_stat[:, 0:1]

    @pl.when(i == 0)
    def _():
        for j in range(n_k):
            vt_ref[j] = v_ref[pl.ds(j * tk, tk), :].T
        dbias_ref[...] = slope * absrel_ref[...]

    q = q_ref[...]
    lane = lax.broadcasted_iota(jnp.int32, q.shape, 1)
    zero = jnp.zeros_like(q)
    qt0 = jnp.where(lane < HEAD_DIM, q, zero).T
    qt1 = jnp.where(lane >= HEAD_DIM, q, zero).T
    q_start = i * tq

    def key_rows(j):
        return pl.ds(pl.multiple_of(j * tk, tk), tk)

    acc0_ref[...] = jnp.zeros_like(acc0_ref)
    acc1_ref[...] = jnp.zeros_like(acc1_ref)
    l0_ref[...] = jnp.zeros_like(l0_ref)
    l1_ref[...] = jnp.zeros_like(l1_ref)

    @pl.when(safe_ref[plan] == 1)
    def _():
        column = lax.broadcasted_iota(jnp.int32, (1, tq), 1)
        qpos = q_start + column
        q_hi = (qpos // POS_SPLIT * POS_SPLIT).astype(F32)
        q_lo = (qpos % POS_SPLIT).astype(F32)
        row = lax.broadcasted_iota(jnp.int32, (HEAD_WIDTH, tq), 0)

        def query_features(sign, bound):
            feat = jnp.where(row < 2, sign * slope,
                             jnp.where(row == 2, -sign * slope * q_hi,
                                       jnp.where(row == 3, -sign * slope * q_lo, 0.0)))
            return jnp.where(row == 4, -bound, feat).astype(BF16)

        def score_bound(qt, kmax):
            qf = qt.astype(F32)
            norm = jnp.sqrt(jnp.sum(qf * qf, axis=0, keepdims=True))
            return norm * kmax * SCORE_BOUND_SLACK + SCORE_BOUND_PAD

        m0 = score_bound(qt0, head_stat[:, 1:2])
        m1 = score_bound(qt1, head_stat[:, 2:3])

        def operands(sign):
            return (jnp.concatenate([qt0, query_features(sign, m0)], axis=0),
                    jnp.concatenate([qt1, query_features(sign, m1)], axis=0))

        def probabilities(s, l_ref):
            p = jnp.exp(s)
            l_ref[...] += jnp.sum(p.reshape(tk // 8, 8, tq), axis=0)
            return p.astype(BF16)

        def tile_probabilities(j, qa0, qa1, mixed_block=None):
            ka = jnp.concatenate([k_ref[key_rows(j), :], kf_ref[key_rows(j), :]], axis=1)
            out = []
            for qa, l_ref in ((qa0, l0_ref), (qa1, l1_ref)):
                s = _dot(ka, qa)
                if mixed_block is not None:
                    parts = [s[:, c * tk:(c + 1) * tk] for c in range(tq // tk)]
                    parts[mixed_block] = parts[mixed_block] - dbias_ref[...]
                    s = jnp.concatenate(parts, axis=1) if len(parts) > 1 else parts[0]
                out.append(probabilities(s, l_ref))
            return out

        def accumulate(j):
            vt = vt_ref[j]
            acc0_ref[...] += _dot(vt, p_ref[0])
            acc1_ref[...] += _dot(vt, p_ref[1])

        def stash(p0, p1):
            p_ref[0] = p0
            p_ref[1] = p1

        def run(first, last, sign, pending):
            qa0, qa1 = operands(sign)

            def body(j, pending):
                accumulate(pending)
                stash(*tile_probabilities(j, qa0, qa1))
                return j

            return lax.fori_loop(first, last, body, pending)

        first_diag = q_start // tk
        for block in range(tq // tk):
            sign = jnp.sign((column // tk - block).astype(F32))
            if block > 0:
                accumulate(first_diag + block - 1)
            stash(*tile_probabilities(first_diag + block, *operands(sign), mixed_block=block))
        pending = run(lo_ref[plan], first_diag, 1.0, first_diag + tq // tk - 1)
        pending = run(first_diag + tq // tk, hi_ref[plan] + 1, -1.0, pending)
        accumulate(pending)

    @pl.when(safe_ref[plan] == 0)
    def _():
        rel = (lax.broadcasted_iota(jnp.int32, (tk, tq), 0)
               - lax.broadcasted_iota(jnp.int32, (tk, tq), 1)).astype(F32)

        def alibi(j):
            return slope * jnp.abs(rel + (j * tk - q_start).astype(F32))

        def update(s, m, l, acc_ref, vt):
            m_new = jnp.maximum(m, jnp.max(s, axis=0, keepdims=True))
            alpha = jnp.exp(m - m_new)
            p = jnp.exp(s - m_new)
            l_new = alpha * l + jnp.sum(p, axis=0, keepdims=True)
            acc_ref[...] = alpha * acc_ref[...] + _dot(vt, p.astype(BF16))
            return m_new, l_new

        def body(j, carry):
            m0, l0, m1, l1 = carry
            kt = k_ref[key_rows(j), :]
            bias = alibi(j)
            vt = vt_ref[j]
            m0, l0 = update(_dot(kt, qt0) - bias, m0, l0, acc0_ref, vt)
            m1, l1 = update(_dot(kt, qt1) - bias, m1, l1, acc1_ref, vt)
            return m0, l0, m1, l1

        init_m = jnp.full((1, tq), NEG_BIG, F32)
        init_l = jnp.zeros((1, tq), F32)
        _, l0, _, l1 = lax.fori_loop(0, n_k, body, (init_m, init_l, init_m, init_l))
        l0_ref[0:1, :] = l0
        l1_ref[0:1, :] = l1

    l0 = jnp.sum(l0_ref[...], axis=0, keepdims=True)
    l1 = jnp.sum(l1_ref[...], axis=0, keepdims=True)
    lam = (jnp.exp(jnp.sum(lq1_ref[...] * lk1_ref[...], axis=-1, keepdims=True))
           - jnp.exp(jnp.sum(lq2_ref[...] * lk2_ref[...], axis=-1, keepdims=True)) + lam_init)
    out_t = acc0_ref[...] / l0 - lam * (acc1_ref[...] / l1)
    out = _rms(out_t.T) * gsub_ref[...] * (1.0 - lam_init)
    o_ref[...] = out.astype(BF16)


def _attention(z, batch, seq, layer, slopes, lam_params, g_subln3, *, tq, tk):
    t = z.shape[0]
    n_q = seq // tq
    blocks_per_col = D_MODEL // HEAD_WIDTH
    lo, hi, safe, head_stat = _attention_plan(z, batch, seq, slopes, tq=tq, tk=tk)
    pos = jnp.arange(tk, dtype=F32)
    abs_rel = jnp.abs(pos[:, None] - pos[None, :])
    lam_spec = pl.BlockSpec((None, 1, HEAD_DIM), lambda b, h, i, *_: (layer, 0, 0))
    grid_spec = pltpu.PrefetchScalarGridSpec(
        num_scalar_prefetch=3,
        grid=(batch, N_HEADS, n_q),
        in_specs=[
            pl.BlockSpec((tq, HEAD_WIDTH), lambda b, h, i, *_: (b * n_q + i, COL_Q * blocks_per_col + h)),
            pl.BlockSpec((seq, HEAD_WIDTH), lambda b, h, i, *_: (b, COL_K * blocks_per_col + h)),
            pl.BlockSpec((seq, HEAD_WIDTH), lambda b, h, i, *_: (b, COL_VA * blocks_per_col + h)),
            pl.BlockSpec((seq, LANES), lambda b, h, i, *_: (0, 0)),
            pl.BlockSpec((tk, tk), lambda b, h, i, *_: (0, 0)),
            pl.BlockSpec((None, None, 1, LANES), lambda b, h, i, *_: (b, h, 0, 0)),
            lam_spec, lam_spec, lam_spec, lam_spec,
            pl.BlockSpec((None, 1, HEAD_WIDTH), lambda b, h, i, *_: (layer, 0, 0)),
        ],
        out_specs=pl.BlockSpec((tq, HEAD_WIDTH), lambda b, h, i, *_: (b * n_q + i, h)),
        scratch_shapes=[
            pltpu.VMEM((seq // tk, HEAD_WIDTH, tk), BF16),
            pltpu.VMEM((HEAD_WIDTH, tq), F32),
            pltpu.VMEM((HEAD_WIDTH, tq), F32),
            pltpu.VMEM((8, tq), F32),
            pltpu.VMEM((8, tq), F32),
            pltpu.VMEM((2, tk, tq), BF16),
            pltpu.VMEM((tk, tk), F32),
        ],
    )
    return pl.pallas_call(
        functools.partial(_attn_kernel, tk=tk, lam_init=_lambda_init(layer)),
        out_shape=jax.ShapeDtypeStruct((t, D_MODEL), BF16),
        grid_spec=grid_spec,
        compiler_params=_params(3),
        name="diff_attention",
    )(lo, hi, safe, z, z, z, _key_position_features(seq), abs_rel, head_stat, *lam_params, g_subln3)


def _route(scores_t, biased_t):
    def row(a, e):
        return a[e:e + 1, :]

    grp_scores = []
    for g in range(N_GROUPS):
        a, b, c, d = (row(biased_t, g * EXPERTS_PER_GROUP + k) for k in range(EXPERTS_PER_GROUP))
        hi_ab, lo_ab = jnp.maximum(a, b), jnp.minimum(a, b)
        hi_cd, lo_cd = jnp.maximum(c, d), jnp.minimum(c, d)
        top1 = jnp.maximum(hi_ab, hi_cd)
        top2 = jnp.maximum(jnp.minimum(hi_ab, hi_cd), jnp.maximum(lo_ab, lo_cd))
        grp_scores.append(top1 + top2)
    best = grp_scores[0]
    sel = jnp.zeros_like(best, dtype=jnp.int32)
    for g in range(1, N_GROUPS):
        better = grp_scores[g] > best
        best = jnp.where(better, grp_scores[g], best)
        sel = jnp.where(better, g, sel)

    def pick(a, k):
        out = row(a, k)
        for g in range(1, N_GROUPS):
            out = jnp.where(sel == g, row(a, g * EXPERTS_PER_GROUP + k), out)
        return out

    cand_b = [pick(biased_t, k) for k in range(EXPERTS_PER_GROUP)]
    cand_s = [pick(scores_t, k) for k in range(EXPERTS_PER_GROUP)]
    v0, k0, w0 = cand_b[0], jnp.zeros_like(sel), cand_s[0]
    for k in range(1, EXPERTS_PER_GROUP):
        better = cand_b[k] > v0
        v0 = jnp.where(better, cand_b[k], v0)
        k0 = jnp.where(better, k, k0)
        w0 = jnp.where(better, cand_s[k], w0)
    v1 = jnp.full_like(v0, -jnp.inf)
    k1 = jnp.zeros_like(sel)
    w1 = jnp.zeros_like(w0)
    for k in range(EXPERTS_PER_GROUP):
        better = (k0 != k) & (cand_b[k] > v1)
        v1 = jnp.where(better, cand_b[k], v1)
        k1 = jnp.where(better, k, k1)
        w1 = jnp.where(better, cand_s[k], w1)
    total = w0 + w1
    e0 = sel * EXPERTS_PER_GROUP + k0
    e1 = sel * EXPERTS_PER_GROUP + k1
    return sel, e0, e1, w0 / total, w1 / total


PAIRS = [(a, b) for a in range(EXPERTS_PER_GROUP) for b in range(a + 1, EXPERTS_PER_GROUP)]
N_CLASSES = N_GROUPS * len(PAIRS)
CLASS_ROWS = 32


def _combine_weights(h2, wr_ref, br_ref):
    rows = h2.shape[0]
    scores = jax.nn.sigmoid(_dot(h2, wr_ref[...]))
    biased = scores + br_ref[...]
    sel, e0, e1, w0, w1 = _route(scores.T, biased.T)
    expert = lax.broadcasted_iota(jnp.int32, (LANES, rows), 0)
    comb_t = jnp.where(expert == e0, w0, 0.0) + jnp.where(expert == e1, w1, 0.0)
    lo = jnp.minimum(e0, e1) - sel * EXPERTS_PER_GROUP
    gap = jnp.abs(e0 - e1) - 1
    first_pair = sum(jnp.where(lo == a, PAIRS.index((a, a + 1)), 0) for a in range(1, EXPERTS_PER_GROUP - 1))
    return sel * len(PAIRS) + first_pair + gap, comb_t.T


def _to_token_rows(ref, value):
    for s in range(D_MODEL // LANES):
        ref[:, s, :] = value[:, s * LANES:(s + 1) * LANES]


def _from_token_rows(ref):
    return jnp.concatenate([ref[:, s, :] for s in range(D_MODEL // LANES)], axis=1)


def _post_kernel(u_ref, v_ref, ga_ref, gb_ref, o_ref, x_ref, gt_ref, shf_ref, scf_ref, gffn_ref,
                 ws_ref, bs_ref, wa_ref, wb_ref, wo_ref, wr_ref, br_ref, upper_ref,
                 xmid_ref, h2_ref, route_ref, count_ref, a_ref, base_ref):
    tm = x_ref.shape[0]
    for n in range(tm // CHUNK):
        rows = pl.ds(n * CHUNK, CHUNK)
        for g in range(GM_GROUPS):
            cols = pl.ds(g * GROUP_WIDTH, GROUP_WIDTH)
            mixed = _dot(ws_ref[g], v_ref[rows, cols]) + bs_ref[g]
            a_ref[rows, cols] = (u_ref[rows, cols].astype(F32) * mixed).astype(BF16)
    ya = _dot(a_ref[...], wa_ref[...])
    yb = _dot(o_ref[...], wb_ref[...])
    merged = ga_ref[...].astype(F32) * ya + gb_ref[...].astype(F32) * yb
    x_mid = x_ref[...] + gt_ref[...] * _dot(merged.astype(BF16), wo_ref[...])
    xmid_ref[...] = x_mid
    h2 = _rms(x_mid) * gffn_ref[...]
    h2 = (h2 * (1.0 + scf_ref[...]) + shf_ref[...]).astype(BF16)
    _to_token_rows(h2_ref, h2.astype(F32))

    pair_class, _ = _combine_weights(h2, wr_ref, br_ref)

    @pl.when(pl.program_id(0) == 0)
    def _():
        base_ref[...] = jnp.zeros_like(base_ref)

    classes = lax.broadcasted_iota(jnp.int32, (CLASS_ROWS, tm), 0)
    member = classes == pair_class
    earlier = _dot(member.astype(BF16), upper_ref[...])
    base = base_ref[...]
    rank = jnp.sum(jnp.where(member, earlier + base[:, 0:1], 0.0), axis=0, keepdims=True)
    base = base + jnp.sum(member.astype(F32), axis=1, keepdims=True)
    base_ref[...] = base
    count_ref[...] = base
    row = lax.broadcasted_iota(jnp.int32, (8, tm), 0)
    route_ref[...] = jnp.where(row == 0, pair_class.astype(F32), jnp.where(row == 1, rank, 0.0))


def _post_attention(z, o, x, mod3, mod_base, seq, layer, g_ffn3, w_s, bs_b, w_br_a, w_br_b, w_out,
                    w_router_p, b_router_p, *, tm):
    t, d = x.shape
    tiles_per_seq = seq // tm

    def mod_row(which):
        return lambda i: (mod_base + (i // tiles_per_seq) * N_MOD + which, 0, 0)

    def z_col(col):
        return pl.BlockSpec((tm, d), lambda i: (i, col))

    def layer_mat():
        return pl.BlockSpec((None, d, d), lambda i: (layer, 0, 0))

    token = jnp.arange(tm)
    upper = (token[:, None] < token[None, :]).astype(BF16)
    row_tile = pl.BlockSpec((tm, d), lambda i: (i, 0))
    return pl.pallas_call(
        _post_kernel,
        out_shape=(jax.ShapeDtypeStruct((t, d), F32),
                   jax.ShapeDtypeStruct((t, d // LANES, LANES), F32),
                   jax.ShapeDtypeStruct((t // tm, 8, tm), F32),
                   jax.ShapeDtypeStruct((CLASS_ROWS, LANES), F32)),
        grid=(t // tm,),
        in_specs=[
            z_col(COL_U), z_col(COL_V), z_col(COL_GA), z_col(COL_GB),
            row_tile, row_tile,
            pl.BlockSpec((None, 1, d), mod_row(2)),
            pl.BlockSpec((None, 1, d), mod_row(3)),
            pl.BlockSpec((None, 1, d), mod_row(4)),
            pl.BlockSpec((None, 1, d), lambda i: (layer, 0, 0)),
            pl.BlockSpec((None, GM_GROUPS, CHUNK, CHUNK), lambda i: (layer, 0, 0, 0)),
            pl.BlockSpec((None, GM_GROUPS, CHUNK, GROUP_WIDTH), lambda i: (layer, 0, 0, 0)),
            layer_mat(), layer_mat(), layer_mat(),
            pl.BlockSpec((d, LANES), lambda i: (0, 0)),
            pl.BlockSpec((1, LANES), lambda i: (0, 0)),
            pl.BlockSpec((tm, tm), lambda i: (0, 0)),
        ],
        out_specs=(row_tile,
                   pl.BlockSpec((tm, d // LANES, LANES), lambda i: (i, 0, 0)),
                   pl.BlockSpec((None, 8, tm), lambda i: (i, 0, 0)),
                   pl.BlockSpec((CLASS_ROWS, LANES), lambda i: (0, 0))),
        scratch_shapes=[pltpu.VMEM((tm, d), BF16), pltpu.VMEM((CLASS_ROWS, LANES), F32)],
        compiler_params=_params(1),
        name="post_attention",
    )(z, z, z, z, o, x, mod3, mod3, mod3, g_ffn3, w_s, bs_b, w_br_a, w_br_b, w_out,
      w_router_p, b_router_p, upper)


MOE_ROWS = 256
DISPATCH_ROWS = 512


def _sorted_positions(route, counts, n_tokens):
    token_class = route[:, 0, :].reshape(-1).astype(jnp.int32)
    rank = route[:, 1, :].reshape(-1).astype(jnp.int32)
    counts = counts[:N_CLASSES, 0].astype(jnp.int32)
    padded = (counts + MOE_ROWS - 1) // MOE_ROWS * MOE_ROWS
    ends = jnp.cumsum(padded)
    starts = ends - padded
    n_tiles = n_tokens // MOE_ROWS + N_CLASSES
    position = rank
    for c in range(N_CLASSES):
        position = position + jnp.where(token_class == c, starts[c], 0)
    tile_start = jnp.arange(n_tiles, dtype=jnp.int32) * MOE_ROWS
    tile_class = jnp.zeros((n_tiles,), jnp.int32)
    for c in range(N_CLASSES - 1):
        tile_class = tile_class + (tile_start >= ends[c]).astype(jnp.int32)
    first_expert = tile_class // len(PAIRS) * EXPERTS_PER_GROUP
    pair = tile_class % len(PAIRS)
    expert_a = first_expert + sum(jnp.where(pair == p, a, 0) for p, (a, _) in enumerate(PAIRS))
    expert_b = first_expert + sum(jnp.where(pair == p, b, 0) for p, (_, b) in enumerate(PAIRS))
    rows = min(DISPATCH_ROWS, n_tokens)
    return (expert_a.astype(jnp.int32), expert_b.astype(jnp.int32),
            position.reshape(n_tokens // rows, 1, rows))


def _row_copies(src_hbm, dst, sem, rows):
    return pltpu.make_async_copy(src_hbm.at[pl.ds(0, rows)], dst, sem)


def _dispatch_kernel(pos_ref, h_hbm, init_hbm, x_hbm, sem):
    del init_hbm
    rows = pos_ref.shape[1]
    first = pl.program_id(0) * rows

    def body(r, carry):
        pltpu.make_async_copy(h_hbm.at[first + r], x_hbm.at[pos_ref[0, r]], sem).start()
        return carry
    lax.fori_loop(0, rows, body, 0)
    _row_copies(h_hbm, x_hbm.at[pl.ds(0, rows)], sem, rows).wait()


def _dispatch(h2_rows, position, n_rows):
    n_tokens, sub, _ = h2_rows.shape
    n_steps, _, rows = position.shape
    return pl.pallas_call(
        _dispatch_kernel,
        out_shape=jax.ShapeDtypeStruct((n_rows, sub, LANES), F32),
        grid=(n_steps,),
        in_specs=[
            pl.BlockSpec((None, 1, rows), lambda i: (i, 0, 0), memory_space=pltpu.SMEM),
            pl.BlockSpec(memory_space=pl.ANY),
            pl.BlockSpec(memory_space=pl.ANY),
        ],
        out_specs=pl.BlockSpec(memory_space=pl.ANY),
        scratch_shapes=[pltpu.SemaphoreType.DMA(())],
        input_output_aliases={2: 0},
        compiler_params=_params(1),
        name="moe_dispatch",
    )(position, h2_rows, jnp.zeros((n_rows, sub, LANES), F32))


def _moe_kernel(expert_a_ref, expert_b_ref, x_ref, wr_ref, br_ref,
                wga_ref, wua_ref, wda_ref, wgb_ref, wub_ref, wdb_ref, y_ref):
    g = pl.program_id(0)
    h = _from_token_rows(x_ref).astype(BF16)
    _, comb = _combine_weights(h, wr_ref, br_ref)
    lane = lax.broadcasted_iota(jnp.int32, comb.shape, 1)
    y = jnp.zeros((h.shape[0], D_MODEL), F32)
    for expert, wg_ref, wu_ref, wd_ref in ((expert_a_ref[g], wga_ref, wua_ref, wda_ref),
                                           (expert_b_ref[g], wgb_ref, wub_ref, wdb_ref)):
        gate = _dot(h, wg_ref[...])
        hid = gate * jax.nn.sigmoid(gate) * _dot(h, wu_ref[...])
        weight = jnp.sum(jnp.where(lane == expert, comb, 0.0), axis=-1, keepdims=True)
        y = y + _dot((weight * hid).astype(BF16), wd_ref[...])
    _to_token_rows(y_ref, y)


def _moe(x_sorted, expert_a, expert_b, layer, w_router_p, b_router_p, w_gate, w_up, w_down):
    n_rows, sub, _ = x_sorted.shape
    d = sub * LANES
    row_tile = pl.BlockSpec((MOE_ROWS, sub, LANES), lambda g, ea, eb: (g, 0, 0))

    def weights_of(which, a, b):
        return pl.BlockSpec((None, None, a, b), lambda g, ea, eb: (layer, (ea, eb)[which][g], 0, 0))

    grid_spec = pltpu.PrefetchScalarGridSpec(
        num_scalar_prefetch=2,
        grid=(n_rows // MOE_ROWS,),
        in_specs=[
            row_tile,
            pl.BlockSpec((d, LANES), lambda g, ea, eb: (0, 0)),
            pl.BlockSpec((1, LANES), lambda g, ea, eb: (0, 0)),
            weights_of(0, d, D_EXPERT), weights_of(0, d, D_EXPERT), weights_of(0, D_EXPERT, d),
            weights_of(1, d, D_EXPERT), weights_of(1, d, D_EXPERT), weights_of(1, D_EXPERT, d),
        ],
        out_specs=row_tile,
    )
    return pl.pallas_call(
        _moe_kernel,
        out_shape=jax.ShapeDtypeStruct((n_rows, sub, LANES), F32),
        grid_spec=grid_spec,
        compiler_params=_params(1),
        name="moe_experts",
    )(expert_a, expert_b, x_sorted, w_router_p, b_router_p, w_gate, w_up, w_down, w_gate, w_up, w_down)


def _residual_kernel(pos_ref, pos_next_ref, x_ref, gt_ref, gfin_ref, y_hbm, out_ref, y_buf, sem,
                     *, final_norm):
    i = pl.program_id(0)
    slot = i % 2
    rows = y_buf.shape[1]

    def start_gather(index_ref, to_slot):
        def body(r, carry):
            pltpu.make_async_copy(y_hbm.at[index_ref[0, r]], y_buf.at[to_slot, r], sem.at[to_slot]).start()
            return carry
        lax.fori_loop(0, rows, body, 0)

    @pl.when(i == 0)
    def _():
        start_gather(pos_ref, 0)

    @pl.when(i + 1 < pl.num_programs(0))
    def _():
        start_gather(pos_next_ref, 1 - slot)

    _row_copies(y_hbm, y_buf.at[slot], sem.at[slot], rows).wait()
    x = x_ref[...] + gt_ref[...] * _from_token_rows(y_buf.at[slot])
    if final_norm:
        x = _rms(x) * gfin_ref[...]
    out_ref[...] = x


def _moe_residual(x_mid, y_sorted, position, mod3, mod_base, seq, g_final2, *, final_norm):
    t, d = x_mid.shape
    n_steps, _, rows = position.shape
    tiles_per_seq = seq // rows
    row_tile = pl.BlockSpec((rows, d), lambda i: (i, 0))
    index_block = lambda offset: pl.BlockSpec(
        (None, 1, rows), lambda i: (jnp.minimum(i + offset, n_steps - 1), 0, 0), memory_space=pltpu.SMEM)
    return pl.pallas_call(
        functools.partial(_residual_kernel, final_norm=final_norm),
        out_shape=jax.ShapeDtypeStruct((t, d), F32),
        grid=(n_steps,),
        in_specs=[
            index_block(0), index_block(1),
            row_tile,
            pl.BlockSpec((None, 1, d), lambda i: (mod_base + (i // tiles_per_seq) * N_MOD + 5, 0, 0)),
            pl.BlockSpec((1, d), lambda i: (0, 0)),
            pl.BlockSpec(memory_space=pl.ANY),
        ],
        out_specs=row_tile,
        scratch_shapes=[pltpu.VMEM((2, rows, d // LANES, LANES), F32), pltpu.SemaphoreType.DMA((2,))],
        compiler_params=_params(1),
        name="moe_residual",
    )(position, position, x_mid, mod3, g_final2, y_sorted)


def _tile(seq, target):
    return min(seq, target)


def _trunk(x3, mod3, batch_offset, padded_batch, w, depth):
    batch, seq, d = x3.shape
    x = x3.reshape(batch * seq, d)
    for layer in range(depth):
        mod_base = (layer * padded_batch + batch_offset) * N_MOD
        z = _in_projection(x, mod3, mod_base, seq, layer, w["g_mix"], w["w_in"], w["g_sg"],
                           tm=_tile(seq, 1024))
        o = _attention(z, batch, seq, layer, w["slopes"], w["lam"], w["g_subln"],
                       tq=_tile(seq, 1024), tk=_tile(seq, 512))
        x_mid, h2_rows, route, counts = _post_attention(
            z, o, x, mod3, mod_base, seq, layer, w["g_ffn"], w["w_s"], w["bs_b"], w["w_br_a"],
            w["w_br_b"], w["w_out"], w["w_router"], w["b_router"], tm=_tile(seq, 512))
        expert_a, expert_b, position = _sorted_positions(route, counts, batch * seq)
        x_sorted = _dispatch(h2_rows, position, expert_a.shape[0] * MOE_ROWS)
        y_sorted = _moe(x_sorted, expert_a, expert_b, layer, w["w_router"], w["b_router"],
                        w["w_gate"], w["w_up"], w["w_down"])
        x = _moe_residual(x_mid, y_sorted, position, mod3, mod_base, seq, w["g_final"],
                          final_norm=(layer == depth - 1))
    return x.reshape(batch, seq, d)


def kernel(x_prompt, x_sample, c_prompt, c_sample, w_ada, b_ada, g_mix, w_in, g_sg, w_s, b_s, w_br_a,
           lam_q1, lam_k1, lam_q2, lam_k2, g_subln, w_br_b, w_out, g_ffn, w_router, b_router,
           w_gate, w_up, w_down, g_final):
    depth, d = g_mix.shape
    n_prompt, n_sample = c_prompt.shape[0], c_sample.shape[0]
    padded_batch = -(-(n_prompt + n_sample) // 8) * 8
    c_all = jnp.zeros((padded_batch, d), F32)
    c_all = c_all.at[:n_prompt].set(c_prompt).at[n_prompt:n_prompt + n_sample].set(c_sample)
    mod = _modulations(c_all, w_ada, b_ada)
    mod3 = mod.reshape(depth * padded_batch * N_MOD, 1, d)

    slopes = jnp.exp2(-(8.0 / N_HEADS) * jnp.arange(1, N_HEADS + 1, dtype=F32))
    w = {
        "g_mix": g_mix.reshape(depth, 1, d),
        "w_in": w_in.astype(BF16),
        "g_sg": g_sg.reshape(depth, 1, d),
        "slopes": slopes,
        "lam": tuple(p.reshape(depth, 1, HEAD_DIM) for p in (lam_q1, lam_k1, lam_q2, lam_k2)),
        "g_subln": g_subln.reshape(depth, 1, HEAD_WIDTH),
        "g_ffn": g_ffn.reshape(depth, 1, d),
        "w_s": w_s.astype(BF16),
        "bs_b": jnp.broadcast_to(b_s[..., None], b_s.shape + (GROUP_WIDTH,)),
        "w_br_a": w_br_a.astype(BF16),
        "w_br_b": w_br_b.astype(BF16),
        "w_out": w_out.astype(BF16),
        "w_router": jnp.pad(w_router, ((0, 0), (0, LANES - N_EXPERTS))).astype(BF16),
        "b_router": jnp.pad(b_router, (0, LANES - N_EXPERTS)).reshape(1, LANES),
        "w_gate": w_gate.astype(BF16),
        "w_up": w_up.astype(BF16),
        "w_down": w_down.astype(BF16),
        "g_final": g_final.reshape(1, d),
    }
    y_prompt = _trunk(x_prompt, mod3, 0, padded_batch, w, depth)
    y_sample = _trunk(x_sample, mod3, n_prompt, padded_batch, w, depth)
    return (y_prompt, y_sample)
```

```python
import functools
import math

import jax
import jax.numpy as jnp
from jax import lax
from jax.experimental import pallas as pl
from jax.experimental.pallas import tpu as pltpu

D_MODEL = 1024
N_HEADS = 8
HEAD_DIM = 64
HEAD_WIDTH = 2 * HEAD_DIM
CHUNK = 128
GM_GROUPS = 8
GROUP_WIDTH = D_MODEL // GM_GROUPS
N_EXPERTS = 16
N_GROUPS = 4
EXPERTS_PER_GROUP = N_EXPERTS // N_GROUPS
D_EXPERT = 512
N_MOD = 6
EPS = 1e-6
COL_U, COL_V, COL_Q, COL_K, COL_VA, COL_GA, COL_GB = range(7)
N_COL_BLOCKS = 7
LANES = 128
NEG_BIG = -1e30

F32 = jnp.float32
BF16 = jnp.bfloat16

VMEM_LIMIT = 56 * 1024 * 1024


def _params(n_axes):
    return pltpu.CompilerParams(dimension_semantics=("arbitrary",) * n_axes,
                                vmem_limit_bytes=VMEM_LIMIT)


def _lambda_init(layer):
    return 0.8 - 0.6 * math.exp(-0.3 * layer)


def _dot(a, b):
    return jnp.dot(a, b, preferred_element_type=F32)


def _mod_kernel(c_ref, w_ref, b_ref, o_ref):
    c = c_ref[...]
    a = c * jax.nn.sigmoid(c)
    o_ref[...] = _dot(a.astype(BF16), w_ref[...].astype(BF16)) + b_ref[...]


def _modulations(c_all, w_ada, b_ada):
    depth, d, n = w_ada.shape
    bp = c_all.shape[0]
    tn = n // 4
    return pl.pallas_call(
        _mod_kernel,
        out_shape=jax.ShapeDtypeStruct((depth, bp, n), F32),
        grid=(depth, n // tn),
        in_specs=[
            pl.BlockSpec((bp, d), lambda l, j: (0, 0)),
            pl.BlockSpec((None, d, tn), lambda l, j: (l, 0, j)),
            pl.BlockSpec((None, 1, tn), lambda l, j: (l, 0, j)),
        ],
        out_specs=pl.BlockSpec((None, bp, tn), lambda l, j: (l, 0, j)),
        compiler_params=_params(2),
        name="adaln_mod",
    )(c_all, w_ada, b_ada.reshape(depth, 1, n))


def _rms(x):
    return x * lax.rsqrt(jnp.mean(x * x, axis=-1, keepdims=True) + EPS)


def _in_kernel(x_ref, sh_ref, sc_ref, g_ref, w_ref, gsg_ref, z_ref, h_ref, *, rows):
    j = pl.program_id(1)
    tm = x_ref.shape[0]
    n_chunks = tm // rows

    @pl.when(j == 0)
    def _():
        for r in range(n_chunks):
            sl = pl.ds(r * rows, rows)
            h = _rms(x_ref[sl, :]) * g_ref[...]
            h = h * (1.0 + sc_ref[...]) + sh_ref[...]
            h_ref[sl, :] = h.astype(BF16)

    def project(epilogue):
        for r in range(n_chunks):
            sl = pl.ds(r * rows, rows)
            z_ref[sl, :] = epilogue(_dot(h_ref[sl, :], w_ref[...])).astype(BF16)

    @pl.when(j == COL_U)
    def _():
        project(jax.nn.gelu)

    @pl.when(j == COL_V)
    def _():
        project(lambda t: _rms(jax.nn.gelu(t)) * gsg_ref[...])

    @pl.when(j == COL_Q)
    def _():
        project(lambda t: t * (HEAD_DIM ** -0.5))

    @pl.when((j == COL_K) | (j == COL_VA))
    def _():
        project(lambda t: t)

    @pl.when((j == COL_GA) | (j == COL_GB))
    def _():
        project(jax.nn.sigmoid)


def _in_projection(x, mod3, mod_base, seq, layer, g_mix3, w_in, g_sg3, *, tm):
    t, d = x.shape
    tiles_per_seq = seq // tm

    def mod_row(which):
        return lambda i, j: (mod_base + (i // tiles_per_seq) * N_MOD + which, 0, 0)

    return pl.pallas_call(
        functools.partial(_in_kernel, rows=min(tm, 256)),
        out_shape=jax.ShapeDtypeStruct((t, N_COL_BLOCKS * d), BF16),
        grid=(t // tm, N_COL_BLOCKS),
        in_specs=[
            pl.BlockSpec((tm, d), lambda i, j: (i, 0)),
            pl.BlockSpec((None, 1, d), mod_row(0)),
            pl.BlockSpec((None, 1, d), mod_row(1)),
            pl.BlockSpec((None, 1, d), lambda i, j: (layer, 0, 0)),
            pl.BlockSpec((None, d, d), lambda i, j: (layer, 0, j)),
            pl.BlockSpec((None, 1, d), lambda i, j: (layer, 0, 0)),
        ],
        out_specs=pl.BlockSpec((tm, d), lambda i, j: (i, j)),
        scratch_shapes=[pltpu.VMEM((tm, d), BF16)],
        compiler_params=_params(2),
        name="in_projection",
    )(x, mod3, mod3, g_mix3, w_in, g_sg3)


SCORE_BOUND_SLACK = 1.01
SCORE_BOUND_PAD = 1e-3
EXP_UNDERFLOW = 105.0
SAFE_SCORE_BOUND = 28.0
POS_SPLIT = 256


def _plan_kernel(q_ref, k_ref, sel_ref, slope_ref, r_ref, m_ref, kmax_ref, qn_ref, sf_ref, kn_ref, *, tq):
    t = pl.program_id(1)
    tp = q_ref.shape[0]
    q = q_ref[...].astype(F32)
    k = k_ref[...].astype(F32)

    def group_sums(x):
        hi = x.astype(BF16)
        lo = (x - hi.astype(F32)).astype(BF16)
        return _dot(hi, sel_ref[...]) + _dot(lo, sel_ref[...])

    rows = pl.ds(pl.multiple_of(t * tp, tp), tp)
    qn_ref[rows, :] = group_sums(q * q)
    sf_ref[rows, :] = group_sums(q * k)
    kn = jnp.max(group_sums(k * k), axis=0, keepdims=True)

    @pl.when(t == 0)
    def _():
        kn_ref[...] = jnp.zeros_like(kn_ref)

    kn_ref[...] = jnp.maximum(kn_ref[...], kn)

    @pl.when(t == pl.num_programs(1) - 1)
    def _():
        kmax = jnp.sqrt(kn_ref[...])
        kmax_ref[...] = jnp.broadcast_to(kmax, kmax_ref.shape)

        def tile(i, carry):
            tile_rows = pl.ds(pl.multiple_of(i * tq, tq), tq)
            m = jnp.sqrt(qn_ref[tile_rows, :]) * kmax * SCORE_BOUND_SLACK + SCORE_BOUND_PAD
            r = (m - sf_ref[tile_rows, :] + EXP_UNDERFLOW) / slope_ref[...]
            m_ref[pl.ds(i, 1), :] = jnp.max(m, axis=0, keepdims=True)
            r_ref[pl.ds(i, 1), :] = jnp.max(r, axis=0, keepdims=True)
            return carry

        lax.fori_loop(0, qn_ref.shape[0] // tq, tile, 0)


def _attention_plan(z, batch, seq, slopes, *, tq, tk):
    n_q, n_k = seq // tq, seq // tk
    tp = min(seq, 1024)
    col = jnp.arange(D_MODEL) // HEAD_DIM
    sel = (col[:, None] == jnp.arange(LANES)[None, :]).astype(BF16)
    slope_row = jnp.ones((LANES,), F32).at[:2 * N_HEADS].set(jnp.repeat(slopes, 2)).reshape(1, LANES)
    stat = jax.ShapeDtypeStruct((batch, n_q, LANES), F32)
    r, m, kmax = pl.pallas_call(
        functools.partial(_plan_kernel, tq=tq),
        out_shape=(stat, stat, jax.ShapeDtypeStruct((batch, 8, LANES), F32)),
        grid=(batch, seq // tp),
        in_specs=[
            pl.BlockSpec((tp, D_MODEL), lambda b, t: (b * (seq // tp) + t, COL_Q)),
            pl.BlockSpec((tp, D_MODEL), lambda b, t: (b * (seq // tp) + t, COL_K)),
            pl.BlockSpec((D_MODEL, LANES), lambda b, t: (0, 0)),
            pl.BlockSpec((1, LANES), lambda b, t: (0, 0)),
        ],
        out_specs=(pl.BlockSpec((None, n_q, LANES), lambda b, t: (b, 0, 0)),
                   pl.BlockSpec((None, n_q, LANES), lambda b, t: (b, 0, 0)),
                   pl.BlockSpec((None, 8, LANES), lambda b, t: (b, 0, 0))),
        scratch_shapes=[pltpu.VMEM((seq, LANES), F32), pltpu.VMEM((seq, LANES), F32),
                        pltpu.VMEM((1, LANES), F32)],
        compiler_params=_params(2),
        name="attention_plan",
    )(z, z, sel, slope_row)

    def per_head(a):
        return jnp.max(a[..., :2 * N_HEADS].reshape(batch, n_q, N_HEADS, 2), axis=-1).transpose(0, 2, 1)

    r, m = per_head(r), per_head(m)
    r = jnp.where(r == r, r, jnp.inf)
    q0 = (jnp.arange(n_q) * tq).astype(F32)[None, None, :]
    first_diag = (jnp.arange(n_q) * tq // tk)[None, None, :]
    last_diag = first_diag + (tq // tk - 1)
    lo = jnp.clip(jnp.ceil((q0 + 1.0 - r) / tk - 1.0), 0, n_k).astype(jnp.int32)
    hi = jnp.clip(jnp.floor((r + q0 + (tq - 1.0)) / tk), 0, n_k - 1).astype(jnp.int32)
    lo = jnp.minimum(lo, first_diag)
    hi = jnp.maximum(hi, last_diag)
    safe = (m <= SAFE_SCORE_BOUND).astype(jnp.int32)
    head_stat = jnp.zeros((batch, N_HEADS, 1, LANES), F32)
    head_stat = head_stat.at[:, :, 0, 0].set(jnp.broadcast_to(slopes, (batch, N_HEADS)))
    head_stat = head_stat.at[:, :, 0, 1].set(kmax[:, 0, 0:2 * N_HEADS:2])
    head_stat = head_stat.at[:, :, 0, 2].set(kmax[:, 0, 1:2 * N_HEADS:2])
    return lo.reshape(-1), hi.reshape(-1), safe.reshape(-1), head_stat


def _key_position_features(seq):
    pos = jnp.arange(seq)
    feat = jnp.zeros((seq, LANES), F32)
    feat = feat.at[:, 0].set((pos // POS_SPLIT * POS_SPLIT).astype(F32))
    feat = feat.at[:, 1].set((pos % POS_SPLIT).astype(F32))
    feat = feat.at[:, 2:5].set(1.0)
    return feat.astype(BF16)


def _attn_kernel(lo_ref, hi_ref, safe_ref, q_ref, k_ref, v_ref, kf_ref, absrel_ref, hs_ref,
                 lq1_ref, lk1_ref, lq2_ref, lk2_ref, gsub_ref,
                 o_ref, vt_ref, acc0_ref, acc1_ref, l0_ref, l1_ref, p_ref, dbias_ref, *, tk, lam_init):
    b, h, i = pl.program_id(0), pl.program_id(1), pl.program_id(2)
    plan = (b * N_HEADS + h) * pl.num_programs(2) + i
    tq = q_ref.shape[0]
    n_k = k_ref.shape[0] // tk

    head_stat = hs_ref[...]
    slope = head_stat[:, 0:1]

    @pl.when(i == 0)
    def _():
        for j in range(n_k):
            vt_ref[j] = v_ref[pl.ds(j * tk, tk), :].T
        dbias_ref[...] = slope * absrel_ref[...]

    q = q_ref[...]
    lane = lax.broadcasted_iota(jnp.int32, q.shape, 1)
    zero = jnp.zeros_like(q)
    qt0 = jnp.where(lane < HEAD_DIM, q, zero).T
    qt1 = jnp.where(lane >= HEAD_DIM, q, zero).T
    q_start = i * tq

    def key_rows(j):
        return pl.ds(pl.multiple_of(j * tk, tk), tk)

    acc0_ref[...] = jnp.zeros_like(acc0_ref)
    acc1_ref[...] = jnp.zeros_like(acc1_ref)
    l0_ref[...] = jnp.zeros_like(l0_ref)
    l1_ref[...] = jnp.zeros_like(l1_ref)

    @pl.when(safe_ref[plan] == 1)
    def _():
        column = lax.broadcasted_iota(jnp.int32, (1, tq), 1)
        qpos = q_start + column
        q_hi = (qpos // POS_SPLIT * POS_SPLIT).astype(F32)
        q_lo = (qpos % POS_SPLIT).astype(F32)
        row = lax.broadcasted_iota(jnp.int32, (HEAD_WIDTH, tq), 0)

        def query_features(sign, bound):
            feat = jnp.where(row < 2, sign * slope,
                             jnp.where(row == 2, -sign * slope * q_hi,
                                       jnp.where(row == 3, -sign * slope * q_lo, 0.0)))
            return jnp.where(row == 4, -bound, feat).astype(BF16)

        def score_bound(qt, kmax):
            qf = qt.astype(F32)
            norm = jnp.sqrt(jnp.sum(qf * qf, axis=0, keepdims=True))
            return norm * kmax * SCORE_BOUND_SLACK + SCORE_BOUND_PAD

        m0 = score_bound(qt0, head_stat[:, 1:2])
        m1 = score_bound(qt1, head_stat[:, 2:3])

        def operands(sign):
            return (jnp.concatenate([qt0, query_features(sign, m0)], axis=0),
                    jnp.concatenate([qt1, query_features(sign, m1)], axis=0))

        def probabilities(s, l_ref):
            p = jnp.exp(s)
            l_ref[...] += jnp.sum(p.reshape(tk // 8, 8, tq), axis=0)
            return p.astype(BF16)

        def tile_probabilities(j, qa0, qa1, mixed_block=None):
            ka = jnp.concatenate([k_ref[key_rows(j), :], kf_ref[key_rows(j), :]], axis=1)
            out = []
            for qa, l_ref in ((qa0, l0_ref), (qa1, l1_ref)):
                s = _dot(ka, qa)
                if mixed_block is not None:
                    parts = [s[:, c * tk:(c + 1) * tk] for c in range(tq // tk)]
                    parts[mixed_block] = parts[mixed_block] - dbias_ref[...]
                    s = jnp.concatenate(parts, axis=1) if len(parts) > 1 else parts[0]
                out.append(probabilities(s, l_ref))
            return out

        def accumulate(j):
            vt = vt_ref[j]
            acc0_ref[...] += _dot(vt, p_ref[0])
            acc1_ref[...] += _dot(vt, p_ref[1])

        def stash(p0, p1):
            p_ref[0] = p0
            p_ref[1] = p1

        def run(first, last, sign, pending):
            qa0, qa1 = operands(sign)

            def body(j, pending):
                accumulate(pending)
                stash(*tile_probabilities(j, qa0, qa1))
                return j

            return lax.fori_loop(first, last, body, pending)

        first_diag = q_start // tk
        for block in range(tq // tk):
            sign = jnp.sign((column // tk - block).astype(F32))
            if block > 0:
                accumulate(first_diag + block - 1)
            stash(*tile_probabilities(first_diag + block, *operands(sign), mixed_block=block))
        pending = run(lo_ref[plan], first_diag, 1.0, first_diag + tq // tk - 1)
        pending = run(first_diag + tq // tk, hi_ref[plan] + 1, -1.0, pending)
        accumulate(pending)

    @pl.when(safe_ref[plan] == 0)
    def _():
        rel = (lax.broadcasted_iota(jnp.int32, (tk, tq), 0)
               - lax.broadcasted_iota(jnp.int32, (tk, tq), 1)).astype(F32)

        def alibi(j):
            return slope * jnp.abs(rel + (j * tk - q_start).astype(F32))

        def update(s, m, l, acc_ref, vt):
            m_new = jnp.maximum(m, jnp.max(s, axis=0, keepdims=True))
            alpha = jnp.exp(m - m_new)
            p = jnp.exp(s - m_new)
            l_new = alpha * l + jnp.sum(p, axis=0, keepdims=True)
            acc_ref[...] = alpha * acc_ref[...] + _dot(vt, p.astype(BF16))
            return m_new, l_new

        def body(j, carry):
            m0, l0, m1, l1 = carry
            kt = k_ref[key_rows(j), :]
            bias = alibi(j)
            vt = vt_ref[j]
            m0, l0 = update(_dot(kt, qt0) - bias, m0, l0, acc0_ref, vt)
            m1, l1 = update(_dot(kt, qt1) - bias, m1, l1, acc1_ref, vt)
            return m0, l0, m1, l1

        init_m = jnp.full((1, tq), NEG_BIG, F32)
        init_l = jnp.zeros((1, tq), F32)
        _, l0, _, l1 = lax.fori_loop(0, n_k, body, (init_m, init_l, init_m, init_l))
        l0_ref[0:1, :] = l0
        l1_ref[0:1, :] = l1

    l0 = jnp.sum(l0_ref[...], axis=0, keepdims=True)
    l1 = jnp.sum(l1_ref[...], axis=0, keepdims=True)
    lam = (jnp.exp(jnp.sum(lq1_ref[...] * lk1_ref[...], axis=-1, keepdims=True))
           - jnp.exp(jnp.sum(lq2_ref[...] * lk2_ref[...], axis=-1, keepdims=True)) + lam_init)
    out_t = acc0_ref[...] / l0 - lam * (acc1_ref[...] / l1)
    out = _rms(out_t.T) * gsub_ref[...] * (1.0 - lam_init)
    o_ref[...] = out.astype(BF16)


def _attention(z, batch, seq, layer, slopes, lam_params, g_subln3, *, tq, tk):
    t = z.shape[0]
    n_q = seq // tq
    blocks_per_col = D_MODEL // HEAD_WIDTH
    lo, hi, safe, head_stat = _attention_plan(z, batch, seq, slopes, tq=tq, tk=tk)
    pos = jnp.arange(tk, dtype=F32)
    abs_rel = jnp.abs(pos[:, None] - pos[None, :])
    lam_spec = pl.BlockSpec((None, 1, HEAD_DIM), lambda b, h, i, *_: (layer, 0, 0))
    grid_spec = pltpu.PrefetchScalarGridSpec(
        num_scalar_prefetch=3,
        grid=(batch, N_HEADS, n_q),
        in_specs=[
            pl.BlockSpec((tq, HEAD_WIDTH), lambda b, h, i, *_: (b * n_q + i, COL_Q * blocks_per_col + h)),
            pl.BlockSpec((seq, HEAD_WIDTH), lambda b, h, i, *_: (b, COL_K * blocks_per_col + h)),
            pl.BlockSpec((seq, HEAD_WIDTH), lambda b, h, i, *_: (b, COL_VA * blocks_per_col + h)),
            pl.BlockSpec((seq, LANES), lambda b, h, i, *_: (0, 0)),
            pl.BlockSpec((tk, tk), lambda b, h, i, *_: (0, 0)),
            pl.BlockSpec((None, None, 1, LANES), lambda b, h, i, *_: (b, h, 0, 0)),
            lam_spec, lam_spec, lam_spec, lam_spec,
            pl.BlockSpec((None, 1, HEAD_WIDTH), lambda b, h, i, *_: (layer, 0, 0)),
        ],
        out_specs=pl.BlockSpec((tq, HEAD_WIDTH), lambda b, h, i, *_: (b * n_q + i, h)),
        scratch_shapes=[
            pltpu.VMEM((seq // tk, HEAD_WIDTH, tk), BF16),
            pltpu.VMEM((HEAD_WIDTH, tq), F32),
            pltpu.VMEM((HEAD_WIDTH, tq), F32),
            pltpu.VMEM((8, tq), F32),
            pltpu.VMEM((8, tq), F32),
            pltpu.VMEM((2, tk, tq), BF16),
            pltpu.VMEM((tk, tk), F32),
        ],
    )
    return pl.pallas_call(
        functools.partial(_attn_kernel, tk=tk, lam_init=_lambda_init(layer)),
        out_shape=jax.ShapeDtypeStruct((t, D_MODEL), BF16),
        grid_spec=grid_spec,
        compiler_params=_params(3),
        name="diff_attention",
    )(lo, hi, safe, z, z, z, _key_position_features(seq), abs_rel, head_stat, *lam_params, g_subln3)


def _route(scores_t, biased_t):
    def row(a, e):
        return a[e:e + 1, :]

    grp_scores = []
    for g in range(N_GROUPS):
        a, b, c, d = (row(biased_t, g * EXPERTS_PER_GROUP + k) for k in range(EXPERTS_PER_GROUP))
        hi_ab, lo_ab = jnp.maximum(a, b), jnp.minimum(a, b)
        hi_cd, lo_cd = jnp.maximum(c, d), jnp.minimum(c, d)
        top1 = jnp.maximum(hi_ab, hi_cd)
        top2 = jnp.maximum(jnp.minimum(hi_ab, hi_cd), jnp.maximum(lo_ab, lo_cd))
        grp_scores.append(top1 + top2)
    best = grp_scores[0]
    sel = jnp.zeros_like(best, dtype=jnp.int32)
    for g in range(1, N_GROUPS):
        better = grp_scores[g] > best
        best = jnp.where(better, grp_scores[g], best)
        sel = jnp.where(better, g, sel)

    def pick(a, k):
        out = row(a, k)
        for g in range(1, N_GROUPS):
            out = jnp.where(sel == g, row(a, g * EXPERTS_PER_GROUP + k), out)
        return out

    cand_b = [pick(biased_t, k) for k in range(EXPERTS_PER_GROUP)]
    cand_s = [pick(scores_t, k) for k in range(EXPERTS_PER_GROUP)]
    v0, k0, w0 = cand_b[0], jnp.zeros_like(sel), cand_s[0]
    for k in range(1, EXPERTS_PER_GROUP):
        better = cand_b[k] > v0
        v0 = jnp.where(better, cand_b[k], v0)
        k0 = jnp.where(better, k, k0)
        w0 = jnp.where(better, cand_s[k], w0)
    v1 = jnp.full_like(v0, -jnp.inf)
    k1 = jnp.zeros_like(sel)
    w1 = jnp.zeros_like(w0)
    for k in range(EXPERTS_PER_GROUP):
        better = (k0 != k) & (cand_b[k] > v1)
        v1 = jnp.where(better, cand_b[k], v1)
        k1 = jnp.where(better, k, k1)
        w1 = jnp.where(better, cand_s[k], w1)
    total = w0 + w1
    e0 = sel * EXPERTS_PER_GROUP + k0
    e1 = sel * EXPERTS_PER_GROUP + k1
    return sel, e0, e1, w0 / total, w1 / total


PAIRS = [(a, b) for a in range(EXPERTS_PER_GROUP) for b in range(a + 1, EXPERTS_PER_GROUP)]
N_CLASSES = N_GROUPS * len(PAIRS)
CLASS_ROWS = 32


def _combine_weights(h2, wr_ref, br_ref):
    rows = h2.shape[0]
    scores = jax.nn.sigmoid(_dot(h2, wr_ref[...]))
    biased = scores + br_ref[...]
    sel, e0, e1, w0, w1 = _route(scores.T, biased.T)
    expert = lax.broadcasted_iota(jnp.int32, (LANES, rows), 0)
    comb_t = jnp.where(expert == e0, w0, 0.0) + jnp.where(expert == e1, w1, 0.0)
    lo = jnp.minimum(e0, e1) - sel * EXPERTS_PER_GROUP
    gap = jnp.abs(e0 - e1) - 1
    first_pair = sum(jnp.where(lo == a, PAIRS.index((a, a + 1)), 0) for a in range(1, EXPERTS_PER_GROUP - 1))
    return sel * len(PAIRS) + first_pair + gap, comb_t.T


def _to_token_rows(ref, value):
    for s in range(D_MODEL // LANES):
        ref[:, s, :] = value[:, s * LANES:(s + 1) * LANES]


def _from_token_rows(ref):
    return jnp.concatenate([ref[:, s, :] for s in range(D_MODEL // LANES)], axis=1)


def _post_kernel(u_ref, v_ref, ga_ref, gb_ref, o_ref, x_ref, gt_ref, shf_ref, scf_ref, gffn_ref,
                 ws_ref, bs_ref, wa_ref, wb_ref, wo_ref, wr_ref, br_ref, upper_ref,
                 xmid_ref, h2_ref, route_ref, count_ref, a_ref, base_ref):
    tm = x_ref.shape[0]
    for n in range(tm // CHUNK):
        rows = pl.ds(n * CHUNK, CHUNK)
        for g in range(GM_GROUPS):
            cols = pl.ds(g * GROUP_WIDTH, GROUP_WIDTH)
            mixed = _dot(ws_ref[g], v_ref[rows, cols]) + bs_ref[g]
            a_ref[rows, cols] = (u_ref[rows, cols].astype(F32) * mixed).astype(BF16)
    ya = _dot(a_ref[...], wa_ref[...])
    yb = _dot(o_ref[...], wb_ref[...])
    merged = ga_ref[...].astype(F32) * ya + gb_ref[...].astype(F32) * yb
    x_mid = x_ref[...] + gt_ref[...] * _dot(merged.astype(BF16), wo_ref[...])
    xmid_ref[...] = x_mid
    h2 = _rms(x_mid) * gffn_ref[...]
    h2 = (h2 * (1.0 + scf_ref[...]) + shf_ref[...]).astype(BF16)
    _to_token_rows(h2_ref, h2.astype(F32))

    pair_class, _ = _combine_weights(h2, wr_ref, br_ref)

    @pl.when(pl.program_id(0) == 0)
    def _():
        base_ref[...] = jnp.zeros_like(base_ref)

    classes = lax.broadcasted_iota(jnp.int32, (CLASS_ROWS, tm), 0)
    member = classes == pair_class
    earlier = _dot(member.astype(BF16), upper_ref[...])
    base = base_ref[...]
    rank = jnp.sum(jnp.where(member, earlier + base[:, 0:1], 0.0), axis=0, keepdims=True)
    base = base + jnp.sum(member.astype(F32), axis=1, keepdims=True)
    base_ref[...] = base
    count_ref[...] = base
    row = lax.broadcasted_iota(jnp.int32, (8, tm), 0)
    route_ref[...] = jnp.where(row == 0, pair_class.astype(F32), jnp.where(row == 1, rank, 0.0))


def _post_attention(z, o, x, mod3, mod_base, seq, layer, g_ffn3, w_s, bs_b, w_br_a, w_br_b, w_out,
                    w_router_p, b_router_p, *, tm):
    t, d = x.shape
    tiles_per_seq = seq // tm

    def mod_row(which):
        return lambda i: (mod_base + (i // tiles_per_seq) * N_MOD + which, 0, 0)

    def z_col(col):
        return pl.BlockSpec((tm, d), lambda i: (i, col))

    def layer_mat():
        return pl.BlockSpec((None, d, d), lambda i: (layer, 0, 0))

    token = jnp.arange(tm)
    upper = (token[:, None] < token[None, :]).astype(BF16)
    row_tile = pl.BlockSpec((tm, d), lambda i: (i, 0))
    return pl.pallas_call(
        _post_kernel,
        out_shape=(jax.ShapeDtypeStruct((t, d), F32),
                   jax.ShapeDtypeStruct((t, d // LANES, LANES), F32),
                   jax.ShapeDtypeStruct((t // tm, 8, tm), F32),
                   jax.ShapeDtypeStruct((CLASS_ROWS, LANES), F32)),
        grid=(t // tm,),
        in_specs=[
            z_col(COL_U), z_col(COL_V), z_col(COL_GA), z_col(COL_GB),
            row_tile, row_tile,
            pl.BlockSpec((None, 1, d), mod_row(2)),
            pl.BlockSpec((None, 1, d), mod_row(3)),
            pl.BlockSpec((None, 1, d), mod_row(4)),
            pl.BlockSpec((None, 1, d), lambda i: (layer, 0, 0)),
            pl.BlockSpec((None, GM_GROUPS, CHUNK, CHUNK), lambda i: (layer, 0, 0, 0)),
            pl.BlockSpec((None, GM_GROUPS, CHUNK, GROUP_WIDTH), lambda i: (layer, 0, 0, 0)),
            layer_mat(), layer_mat(), layer_mat(),
            pl.BlockSpec((d, LANES), lambda i: (0, 0)),
            pl.BlockSpec((1, LANES), lambda i: (0, 0)),
            pl.BlockSpec((tm, tm), lambda i: (0, 0)),
        ],
        out_specs=(row_tile,
                   pl.BlockSpec((tm, d // LANES, LANES), lambda i: (i, 0, 0)),
                   pl.BlockSpec((None, 8, tm), lambda i: (i, 0, 0)),
                   pl.BlockSpec((CLASS_ROWS, LANES), lambda i: (0, 0))),
        scratch_shapes=[pltpu.VMEM((tm, d), BF16), pltpu.VMEM((CLASS_ROWS, LANES), F32)],
        compiler_params=_params(1),
        name="post_attention",
    )(z, z, z, z, o, x, mod3, mod3, mod3, g_ffn3, w_s, bs_b, w_br_a, w_br_b, w_out,
      w_router_p, b_router_p, upper)


MOE_ROWS = 256
DISPATCH_ROWS = 512


def _sorted_positions(route, counts, n_tokens):
    token_class = route[:, 0, :].reshape(-1).astype(jnp.int32)
    rank = route[:, 1, :].reshape(-1).astype(jnp.int32)
    counts = counts[:N_CLASSES, 0].astype(jnp.int32)
    padded = (counts + MOE_ROWS - 1) // MOE_ROWS * MOE_ROWS
    ends = jnp.cumsum(padded)
    starts = ends - padded
    n_tiles = n_tokens // MOE_ROWS + N_CLASSES
    position = rank
    for c in range(N_CLASSES):
        position = position + jnp.where(token_class == c, starts[c], 0)
    tile_start = jnp.arange(n_tiles, dtype=jnp.int32) * MOE_ROWS
    tile_class = jnp.zeros((n_tiles,), jnp.int32)
    for c in range(N_CLASSES - 1):
        tile_class = tile_class + (tile_start >= ends[c]).astype(jnp.int32)
    first_expert = tile_class // len(PAIRS) * EXPERTS_PER_GROUP
    pair = tile_class % len(PAIRS)
    expert_a = first_expert + sum(jnp.where(pair == p, a, 0) for p, (a, _) in enumerate(PAIRS))
    expert_b = first_expert + sum(jnp.where(pair == p, b, 0) for p, (_, b) in enumerate(PAIRS))
    rows = min(DISPATCH_ROWS, n_tokens)
    return (expert_a.astype(jnp.int32), expert_b.astype(jnp.int32),
            position.reshape(n_tokens // rows, 1, rows))


def _row_copies(src_hbm, dst, sem, rows):
    return pltpu.make_async_copy(src_hbm.at[pl.ds(0, rows)], dst, sem)


def _dispatch_kernel(pos_ref, h_ref, init_hbm, x_hbm, sem):
    del init_hbm
    rows = pos_ref.shape[1]

    def body(r, carry):
        pltpu.make_async_copy(h_ref.at[r], x_hbm.at[pos_ref[0, r]], sem).start()
        return carry
    lax.fori_loop(0, rows, body, 0)
    pltpu.make_async_copy(h_ref, x_hbm.at[pl.ds(0, rows)], sem).wait()


def _dispatch(h2_rows, position, n_rows):
    n_tokens, sub, _ = h2_rows.shape
    n_steps, _, rows = position.shape
    return pl.pallas_call(
        _dispatch_kernel,
        out_shape=jax.ShapeDtypeStruct((n_rows, sub, LANES), F32),
        grid=(n_steps,),
        in_specs=[
            pl.BlockSpec((None, 1, rows), lambda i: (i, 0, 0), memory_space=pltpu.SMEM),
            pl.BlockSpec((rows, sub, LANES), lambda i: (i, 0, 0)),
            pl.BlockSpec(memory_space=pl.ANY),
        ],
        out_specs=pl.BlockSpec(memory_space=pl.ANY),
        scratch_shapes=[pltpu.SemaphoreType.DMA(())],
        input_output_aliases={2: 0},
        compiler_params=_params(1),
        name="moe_dispatch",
    )(position, h2_rows, jnp.zeros((n_rows, sub, LANES), F32))


def _moe_kernel(expert_a_ref, expert_b_ref, x_ref, wr_ref, br_ref,
                wga_ref, wua_ref, wda_ref, wgb_ref, wub_ref, wdb_ref, y_ref):
    g = pl.program_id(0)
    h = _from_token_rows(x_ref).astype(BF16)
    _, comb = _combine_weights(h, wr_ref, br_ref)
    lane = lax.broadcasted_iota(jnp.int32, comb.shape, 1)
    y = jnp.zeros((h.shape[0], D_MODEL), F32)
    for expert, wg_ref, wu_ref, wd_ref in ((expert_a_ref[g], wga_ref, wua_ref, wda_ref),
                                           (expert_b_ref[g], wgb_ref, wub_ref, wdb_ref)):
        gate = _dot(h, wg_ref[...])
        hid = gate * jax.nn.sigmoid(gate) * _dot(h, wu_ref[...])
        weight = jnp.sum(jnp.where(lane == expert, comb, 0.0), axis=-1, keepdims=True)
        y = y + _dot((weight * hid).astype(BF16), wd_ref[...])
    _to_token_rows(y_ref, y)


def _moe(x_sorted, expert_a, expert_b, layer, w_router_p, b_router_p, w_gate, w_up, w_down):
    n_rows, sub, _ = x_sorted.shape
    d = sub * LANES
    row_tile = pl.BlockSpec((MOE_ROWS, sub, LANES), lambda g, ea, eb: (g, 0, 0))

    def weights_of(which, a, b):
        return pl.BlockSpec((None, None, a, b), lambda g, ea, eb: (layer, (ea, eb)[which][g], 0, 0))

    grid_spec = pltpu.PrefetchScalarGridSpec(
        num_scalar_prefetch=2,
        grid=(n_rows // MOE_ROWS,),
        in_specs=[
            row_tile,
            pl.BlockSpec((d, LANES), lambda g, ea, eb: (0, 0)),
            pl.BlockSpec((1, LANES), lambda g, ea, eb: (0, 0)),
            weights_of(0, d, D_EXPERT), weights_of(0, d, D_EXPERT), weights_of(0, D_EXPERT, d),
            weights_of(1, d, D_EXPERT), weights_of(1, d, D_EXPERT), weights_of(1, D_EXPERT, d),
        ],
        out_specs=row_tile,
    )
    return pl.pallas_call(
        _moe_kernel,
        out_shape=jax.ShapeDtypeStruct((n_rows, sub, LANES), F32),
        grid_spec=grid_spec,
        compiler_params=_params(1),
        name="moe_experts",
    )(expert_a, expert_b, x_sorted, w_router_p, b_router_p, w_gate, w_up, w_down, w_gate, w_up, w_down)


def _residual_kernel(pos_ref, pos_next_ref, x_ref, gt_ref, gfin_ref, y_hbm, out_ref, y_buf, sem,
                     *, final_norm):
    i = pl.program_id(0)
    slot = i % 2
    rows = y_buf.shape[1]

    def start_gather(index_ref, to_slot):
        def body(r, carry):
            pltpu.make_async_copy(y_hbm.at[index_ref[0, r]], y_buf.at[to_slot, r], sem.at[to_slot]).start()
            return carry
        lax.fori_loop(0, rows, body, 0)

    @pl.when(i == 0)
    def _():
        start_gather(pos_ref, 0)

    @pl.when(i + 1 < pl.num_programs(0))
    def _():
        start_gather(pos_next_ref, 1 - slot)

    _row_copies(y_hbm, y_buf.at[slot], sem.at[slot], rows).wait()
    x = x_ref[...] + gt_ref[...] * _from_token_rows(y_buf.at[slot])
    if final_norm:
        x = _rms(x) * gfin_ref[...]
    out_ref[...] = x


def _moe_residual(x_mid, y_sorted, position, mod3, mod_base, seq, g_final2, *, final_norm):
    t, d = x_mid.shape
    n_steps, _, rows = position.shape
    tiles_per_seq = seq // rows
    row_tile = pl.BlockSpec((rows, d), lambda i: (i, 0))
    index_block = lambda offset: pl.BlockSpec(
        (None, 1, rows), lambda i: (jnp.minimum(i + offset, n_steps - 1), 0, 0), memory_space=pltpu.SMEM)
    return pl.pallas_call(
        functools.partial(_residual_kernel, final_norm=final_norm),
        out_shape=jax.ShapeDtypeStruct((t, d), F32),
        grid=(n_steps,),
        in_specs=[
            index_block(0), index_block(1),
            row_tile,
            pl.BlockSpec((None, 1, d), lambda i: (mod_base + (i // tiles_per_seq) * N_MOD + 5, 0, 0)),
            pl.BlockSpec((1, d), lambda i: (0, 0)),
            pl.BlockSpec(memory_space=pl.ANY),
        ],
        out_specs=row_tile,
        scratch_shapes=[pltpu.VMEM((2, rows, d // LANES, LANES), F32), pltpu.SemaphoreType.DMA((2,))],
        compiler_params=_params(1),
        name="moe_residual",
    )(position, position, x_mid, mod3, g_final2, y_sorted)


def _tile(seq, target):
    return min(seq, target)


def _trunk(x3, mod3, batch_offset, padded_batch, w, depth):
    batch, seq, d = x3.shape
    x = x3.reshape(batch * seq, d)
    for layer in range(depth):
        mod_base = (layer * padded_batch + batch_offset) * N_MOD
        z = _in_projection(x, mod3, mod_base, seq, layer, w["g_mix"], w["w_in"], w["g_sg"],
                           tm=_tile(seq, 1024))
        o = _attention(z, batch, seq, layer, w["slopes"], w["lam"], w["g_subln"],
                       tq=_tile(seq, 1024), tk=_tile(seq, 512))
        x_mid, h2_rows, route, counts = _post_attention(
            z, o, x, mod3, mod_base, seq, layer, w["g_ffn"], w["w_s"], w["bs_b"], w["w_br_a"],
            w["w_br_b"], w["w_out"], w["w_router"], w["b_router"], tm=_tile(seq, 512))
        expert_a, expert_b, position = _sorted_positions(route, counts, batch * seq)
        x_sorted = _dispatch(h2_rows, position, expert_a.shape[0] * MOE_ROWS)
        y_sorted = _moe(x_sorted, expert_a, expert_b, layer, w["w_router"], w["b_router"],
                        w["w_gate"], w["w_up"], w["w_down"])
        x = _moe_residual(x_mid, y_sorted, position, mod3, mod_base, seq, w["g_final"],
                          final_norm=(layer == depth - 1))
    return x.reshape(batch, seq, d)


def kernel(x_prompt, x_sample, c_prompt, c_sample, w_ada, b_ada, g_mix, w_in, g_sg, w_s, b_s, w_br_a,
           lam_q1, lam_k1, lam_q2, lam_k2, g_subln, w_br_b, w_out, g_ffn, w_router, b_router,
           w_gate, w_up, w_down, g_final):
    depth, d = g_mix.shape
    n_prompt, n_sample = c_prompt.shape[0], c_sample.shape[0]
    padded_batch = -(-(n_prompt + n_sample) // 8) * 8
    c_all = jnp.zeros((padded_batch, d), F32)
    c_all = c_all.at[:n_prompt].set(c_prompt).at[n_prompt:n_prompt + n_sample].set(c_sample)
    mod = _modulations(c_all, w_ada, b_ada)
    mod3 = mod.reshape(depth * padded_batch * N_MOD, 1, d)

    slopes = jnp.exp2(-(8.0 / N_HEADS) * jnp.arange(1, N_HEADS + 1, dtype=F32))
    w = {
        "g_mix": g_mix.reshape(depth, 1, d),
        "w_in": w_in.astype(BF16),
        "g_sg": g_sg.reshape(depth, 1, d),
        "slopes": slopes,
        "lam": tuple(p.reshape(depth, 1, HEAD_DIM) for p in (lam_q1, lam_k1, lam_q2, lam_k2)),
        "g_subln": g_subln.reshape(depth, 1, HEAD_WIDTH),
        "g_ffn": g_ffn.reshape(depth, 1, d),
        "w_s": w_s.astype(BF16),
        "bs_b": jnp.broadcast_to(b_s[..., None], b_s.shape + (GROUP_WIDTH,)),
        "w_br_a": w_br_a.astype(BF16),
        "w_br_b": w_br_b.astype(BF16),
        "w_out": w_out.astype(BF16),
        "w_router": jnp.pad(w_router, ((0, 0), (0, LANES - N_EXPERTS))).astype(BF16),
        "b_router": jnp.pad(b_router, (0, LANES - N_EXPERTS)).reshape(1, LANES),
        "w_gate": w_gate.astype(BF16),
        "w_up": w_up.astype(BF16),
        "w_down": w_down.astype(BF16),
        "g_final": g_final.reshape(1, d),
    }
    y_prompt = _trunk(x_prompt, mod3, 0, padded_batch, w, depth)
    y_sample = _trunk(x_sample, mod3, n_prompt, padded_batch, w, depth)
    return (y_prompt, y_sample)
```

```python
import functools
import math

import jax
import jax.numpy as jnp
from jax import lax
from jax.experimental import pallas as pl
from jax.experimental.pallas import tpu as pltpu

D_MODEL = 1024
N_HEADS = 8
HEAD_DIM = 64
HEAD_WIDTH = 2 * HEAD_DIM
CHUNK = 128
GM_GROUPS = 8
GROUP_WIDTH = D_MODEL // GM_GROUPS
N_EXPERTS = 16
N_GROUPS = 4
EXPERTS_PER_GROUP = N_EXPERTS // N_GROUPS
D_EXPERT = 512
N_MOD = 6
EPS = 1e-6
COL_U, COL_V, COL_Q, COL_K, COL_VA, COL_GA, COL_GB = range(7)
N_COL_BLOCKS = 7
LANES = 128
NEG_BIG = -1e30

F32 = jnp.float32
BF16 = jnp.bfloat16

VMEM_LIMIT = 56 * 1024 * 1024


def _params(n_axes):
    return pltpu.CompilerParams(dimension_semantics=("arbitrary",) * n_axes,
                                vmem_limit_bytes=VMEM_LIMIT)


def _lambda_init(layer):
    return 0.8 - 0.6 * math.exp(-0.3 * layer)


def _dot(a, b):
    return jnp.dot(a, b, preferred_element_type=F32)


def _mod_kernel(c_ref, w_ref, b_ref, o_ref):
    c = c_ref[...]
    a = c * jax.nn.sigmoid(c)
    o_ref[...] = _dot(a.astype(BF16), w_ref[...].astype(BF16)) + b_ref[...]


def _modulations(c_all, w_ada, b_ada):
    depth, d, n = w_ada.shape
    bp = c_all.shape[0]
    tn = n // 4
    return pl.pallas_call(
        _mod_kernel,
        out_shape=jax.ShapeDtypeStruct((depth, bp, n), F32),
        grid=(depth, n // tn),
        in_specs=[
            pl.BlockSpec((bp, d), lambda l, j: (0, 0)),
            pl.BlockSpec((None, d, tn), lambda l, j: (l, 0, j)),
            pl.BlockSpec((None, 1, tn), lambda l, j: (l, 0, j)),
        ],
        out_specs=pl.BlockSpec((None, bp, tn), lambda l, j: (l, 0, j)),
        compiler_params=_params(2),
        name="adaln_mod",
    )(c_all, w_ada, b_ada.reshape(depth, 1, n))


def _rms(x):
    return x * lax.rsqrt(jnp.mean(x * x, axis=-1, keepdims=True) + EPS)


def _in_kernel(x_ref, sh_ref, sc_ref, g_ref, w_ref, gsg_ref, z_ref, h_ref, *, rows):
    j = pl.program_id(1)
    tm = x_ref.shape[0]
    n_chunks = tm // rows

    @pl.when(j == 0)
    def _():
        for r in range(n_chunks):
            sl = pl.ds(r * rows, rows)
            h = _rms(x_ref[sl, :]) * g_ref[...]
            h = h * (1.0 + sc_ref[...]) + sh_ref[...]
            h_ref[sl, :] = h.astype(BF16)

    def project(epilogue):
        for r in range(n_chunks):
            sl = pl.ds(r * rows, rows)
            z_ref[sl, :] = epilogue(_dot(h_ref[sl, :], w_ref[...])).astype(BF16)

    @pl.when(j == COL_U)
    def _():
        project(jax.nn.gelu)

    @pl.when(j == COL_V)
    def _():
        project(lambda t: _rms(jax.nn.gelu(t)) * gsg_ref[...])

    @pl.when(j == COL_Q)
    def _():
        project(lambda t: t * (HEAD_DIM ** -0.5))

    @pl.when((j == COL_K) | (j == COL_VA))
    def _():
        project(lambda t: t)

    @pl.when((j == COL_GA) | (j == COL_GB))
    def _():
        project(jax.nn.sigmoid)


def _in_projection(x, mod3, mod_base, seq, layer, g_mix3, w_in, g_sg3, *, tm):
    t, d = x.shape
    tiles_per_seq = seq // tm

    def mod_row(which):
        return lambda i, j: (mod_base + (i // tiles_per_seq) * N_MOD + which, 0, 0)

    return pl.pallas_call(
        functools.partial(_in_kernel, rows=min(tm, 256)),
        out_shape=jax.ShapeDtypeStruct((t, N_COL_BLOCKS * d), BF16),
        grid=(t // tm, N_COL_BLOCKS),
        in_specs=[
            pl.BlockSpec((tm, d), lambda i, j: (i, 0)),
            pl.BlockSpec((None, 1, d), mod_row(0)),
            pl.BlockSpec((None, 1, d), mod_row(1)),
            pl.BlockSpec((None, 1, d), lambda i, j: (layer, 0, 0)),
            pl.BlockSpec((None, d, d), lambda i, j: (layer, 0, j)),
            pl.BlockSpec((None, 1, d), lambda i, j: (layer, 0, 0)),
        ],
        out_specs=pl.BlockSpec((tm, d), lambda i, j: (i, j)),
        scratch_shapes=[pltpu.VMEM((tm, d), BF16)],
        compiler_params=_params(2),
        name="in_projection",
    )(x, mod3, mod3, g_mix3, w_in, g_sg3)


SCORE_BOUND_SLACK = 1.01
SCORE_BOUND_PAD = 1e-3
EXP_UNDERFLOW = 105.0
SAFE_SCORE_BOUND = 28.0
POS_SPLIT = 256


def _plan_kernel(q_ref, k_ref, sel_ref, slope_ref, r_ref, m_ref, kmax_ref, qn_ref, sf_ref, kn_ref, *, tq):
    t = pl.program_id(1)
    tp = q_ref.shape[0]
    q = q_ref[...].astype(F32)
    k = k_ref[...].astype(F32)

    def group_sums(x):
        hi = x.astype(BF16)
        lo = (x - hi.astype(F32)).astype(BF16)
        return _dot(hi, sel_ref[...]) + _dot(lo, sel_ref[...])

    rows = pl.ds(pl.multiple_of(t * tp, tp), tp)
    qn_ref[rows, :] = group_sums(q * q)
    sf_ref[rows, :] = group_sums(q * k)
    kn = jnp.max(group_sums(k * k), axis=0, keepdims=True)

    @pl.when(t == 0)
    def _():
        kn_ref[...] = jnp.zeros_like(kn_ref)

    kn_ref[...] = jnp.maximum(kn_ref[...], kn)

    @pl.when(t == pl.num_programs(1) - 1)
    def _():
        kmax = jnp.sqrt(kn_ref[...])
        kmax_ref[...] = jnp.broadcast_to(kmax, kmax_ref.shape)

        def tile(i, carry):
            tile_rows = pl.ds(pl.multiple_of(i * tq, tq), tq)
            m = jnp.sqrt(qn_ref[tile_rows, :]) * kmax * SCORE_BOUND_SLACK + SCORE_BOUND_PAD
            r = (m - sf_ref[tile_rows, :] + EXP_UNDERFLOW) / slope_ref[...]
            m_ref[pl.ds(i, 1), :] = jnp.max(m, axis=0, keepdims=True)
            r_ref[pl.ds(i, 1), :] = jnp.max(r, axis=0, keepdims=True)
            return carry

        lax.fori_loop(0, qn_ref.shape[0] // tq, tile, 0)


def _attention_plan(z, batch, seq, slopes, *, tq, tk):
    n_q, n_k = seq // tq, seq // tk
    tp = min(seq, 1024)
    col = jnp.arange(D_MODEL) // HEAD_DIM
    sel = (col[:, None] == jnp.arange(LANES)[None, :]).astype(BF16)
    slope_row = jnp.ones((LANES,), F32).at[:2 * N_HEADS].set(jnp.repeat(slopes, 2)).reshape(1, LANES)
    stat = jax.ShapeDtypeStruct((batch, n_q, LANES), F32)
    r, m, kmax = pl.pallas_call(
        functools.partial(_plan_kernel, tq=tq),
        out_shape=(stat, stat, jax.ShapeDtypeStruct((batch, 8, LANES), F32)),
        grid=(batch, seq // tp),
        in_specs=[
            pl.BlockSpec((tp, D_MODEL), lambda b, t: (b * (seq // tp) + t, COL_Q)),
            pl.BlockSpec((tp, D_MODEL), lambda b, t: (b * (seq // tp) + t, COL_K)),
            pl.BlockSpec((D_MODEL, LANES), lambda b, t: (0, 0)),
            pl.BlockSpec((1, LANES), lambda b, t: (0, 0)),
        ],
        out_specs=(pl.BlockSpec((None, n_q, LANES), lambda b, t: (b, 0, 0)),
                   pl.BlockSpec((None, n_q, LANES), lambda b, t: (b, 0, 0)),
                   pl.BlockSpec((None, 8, LANES), lambda b, t: (b, 0, 0))),
        scratch_shapes=[pltpu.VMEM((seq, LANES), F32), pltpu.VMEM((seq, LANES), F32),
                        pltpu.VMEM((1, LANES), F32)],
        compiler_params=_params(2),
        name="attention_plan",
    )(z, z, sel, slope_row)

    def per_head(a):
        return jnp.max(a[..., :2 * N_HEADS].reshape(batch, n_q, N_HEADS, 2), axis=-1).transpose(0, 2, 1)

    r, m = per_head(r), per_head(m)
    r = jnp.where(r == r, r, jnp.inf)
    q0 = (jnp.arange(n_q) * tq).astype(F32)[None, None, :]
    first_diag = (jnp.arange(n_q) * tq // tk)[None, None, :]
    last_diag = first_diag + (tq // tk - 1)
    lo = jnp.clip(jnp.ceil((q0 + 1.0 - r) / tk - 1.0), 0, n_k).astype(jnp.int32)
    hi = jnp.clip(jnp.floor((r + q0 + (tq - 1.0)) / tk), 0, n_k - 1).astype(jnp.int32)
    lo = jnp.minimum(lo, first_diag)
    hi = jnp.maximum(hi, last_diag)
    safe = (m <= SAFE_SCORE_BOUND).astype(jnp.int32)
    head_stat = jnp.zeros((batch, N_HEADS, 1, LANES), F32)
    head_stat = head_stat.at[:, :, 0, 0].set(jnp.broadcast_to(slopes, (batch, N_HEADS)))
    head_stat = head_stat.at[:, :, 0, 1].set(kmax[:, 0, 0:2 * N_HEADS:2])
    head_stat = head_stat.at[:, :, 0, 2].set(kmax[:, 0, 1:2 * N_HEADS:2])
    return lo.reshape(-1), hi.reshape(-1), safe.reshape(-1), head_stat


def _key_position_features(seq):
    pos = jnp.arange(seq)
    feat = jnp.zeros((seq, LANES), F32)
    feat = feat.at[:, 0].set((pos // POS_SPLIT * POS_SPLIT).astype(F32))
    feat = feat.at[:, 1].set((pos % POS_SPLIT).astype(F32))
    feat = feat.at[:, 2:5].set(1.0)
    return feat.astype(BF16)


def _attn_kernel(lo_ref, hi_ref, safe_ref, q_ref, k_ref, v_ref, kf_ref, absrel_ref, hs_ref,
                 lq1_ref, lk1_ref, lq2_ref, lk2_ref, gsub_ref,
                 o_ref, vt_ref, acc0_ref, acc1_ref, l0_ref, l1_ref, p_ref, dbias_ref, *, tk, lam_init):
    b, h, i = pl.program_id(0), pl.program_id(1), pl.program_id(2)
    plan = (b * N_HEADS + h) * pl.num_programs(2) + i
    tq = q_ref.shape[0]
    n_k = k_ref.shape[0] // tk

    head_stat = hs_ref[...]
    slope = head_stat[:, 0:1]

    @pl.when(i == 0)
    def _():
        for j in range(n_k):
            vt_ref[j] = v_ref[pl.ds(j * tk, tk), :].T
        dbias_ref[...] = slope * absrel_ref[...]

    q = q_ref[...]
    lane = lax.broadcasted_iota(jnp.int32, q.shape, 1)
    zero = jnp.zeros_like(q)
    qt0 = jnp.where(lane < HEAD_DIM, q, zero).T
    qt1 = jnp.where(lane >= HEAD_DIM, q, zero).T
    q_start = i * tq

    def key_rows(j):
        return pl.ds(pl.multiple_of(j * tk, tk), tk)

    acc0_ref[...] = jnp.zeros_like(acc0_ref)
    acc1_ref[...] = jnp.zeros_like(acc1_ref)
    l0_ref[...] = jnp.zeros_like(l0_ref)
    l1_ref[...] = jnp.zeros_like(l1_ref)

    @pl.when(safe_ref[plan] == 1)
    def _():
        column = lax.broadcasted_iota(jnp.int32, (1, tq), 1)
        qpos = q_start + column
        q_hi = (qpos // POS_SPLIT * POS_SPLIT).astype(F32)
        q_lo = (qpos % POS_SPLIT).astype(F32)
        row = lax.broadcasted_iota(jnp.int32, (HEAD_WIDTH, tq), 0)

        def query_features(sign, bound):
            feat = jnp.where(row < 2, sign * slope,
                             jnp.where(row == 2, -sign * slope * q_hi,
                                       jnp.where(row == 3, -sign * slope * q_lo, 0.0)))
            return jnp.where(row == 4, -bound, feat).astype(BF16)

        def score_bound(qt, kmax):
            qf = qt.astype(F32)
            norm = jnp.sqrt(jnp.sum(qf * qf, axis=0, keepdims=True))
            return norm * kmax * SCORE_BOUND_SLACK + SCORE_BOUND_PAD

        m0 = score_bound(qt0, head_stat[:, 1:2])
        m1 = score_bound(qt1, head_stat[:, 2:3])

        def operands(sign):
            return (jnp.concatenate([qt0, query_features(sign, m0)], axis=0),
                    jnp.concatenate([qt1, query_features(sign, m1)], axis=0))

        def probabilities(s, l_ref):
            p = jnp.exp(s)
            l_ref[...] += jnp.sum(p.reshape(tk // 8, 8, tq), axis=0)
            return p.astype(BF16)

        def tile_probabilities(j, qa0, qa1, mixed_block=None):
            ka = jnp.concatenate([k_ref[key_rows(j), :], kf_ref[key_rows(j), :]], axis=1)
            out = []
            for qa, l_ref in ((qa0, l0_ref), (qa1, l1_ref)):
                s = _dot(ka, qa)
                if mixed_block is not None:
                    parts = [s[:, c * tk:(c + 1) * tk] for c in range(tq // tk)]
                    parts[mixed_block] = parts[mixed_block] - dbias_ref[...]
                    s = jnp.concatenate(parts, axis=1) if len(parts) > 1 else parts[0]
                out.append(probabilities(s, l_ref))
            return out

        def accumulate(j):
            vt = vt_ref[j]
            acc0_ref[...] += _dot(vt, p_ref[0])
            acc1_ref[...] += _dot(vt, p_ref[1])

        def stash(p0, p1):
            p_ref[0] = p0
            p_ref[1] = p1

        def run(first, last, sign, pending):
            qa0, qa1 = operands(sign)

            def body(j, pending):
                accumulate(pending)
                stash(*tile_probabilities(j, qa0, qa1))
                return j

            return lax.fori_loop(first, last, body, pending)

        first_diag = q_start // tk
        for block in range(tq // tk):
            sign = jnp.sign((column // tk - block).astype(F32))
            if block > 0:
                accumulate(first_diag + block - 1)
            stash(*tile_probabilities(first_diag + block, *operands(sign), mixed_block=block))
        pending = run(lo_ref[plan], first_diag, 1.0, first_diag + tq // tk - 1)
        pending = run(first_diag + tq // tk, hi_ref[plan] + 1, -1.0, pending)
        accumulate(pending)

    @pl.when(safe_ref[plan] == 0)
    def _():
        rel = (lax.broadcasted_iota(jnp.int32, (tk, tq), 0)
               - lax.broadcasted_iota(jnp.int32, (tk, tq), 1)).astype(F32)

        def alibi(j):
            return slope * jnp.abs(rel + (j * tk - q_start).astype(F32))

        def update(s, m, l, acc_ref, vt):
            m_new = jnp.maximum(m, jnp.max(s, axis=0, keepdims=True))
            alpha = jnp.exp(m - m_new)
            p = jnp.exp(s - m_new)
            l_new = alpha * l + jnp.sum(p, axis=0, keepdims=True)
            acc_ref[...] = alpha * acc_ref[...] + _dot(vt, p.astype(BF16))
            return m_new, l_new

        def body(j, carry):
            m0, l0, m1, l1 = carry
            kt = k_ref[key_rows(j), :]
            bias = alibi(j)
            vt = vt_ref[j]
            m0, l0 = update(_dot(kt, qt0) - bias, m0, l0, acc0_ref, vt)
            m1, l1 = update(_dot(kt, qt1) - bias, m1, l1, acc1_ref, vt)
            return m0, l0, m1, l1

        init_m = jnp.full((1, tq), NEG_BIG, F32)
        init_l = jnp.zeros((1, tq), F32)
        _, l0, _, l1 = lax.fori_loop(0, n_k, body, (init_m, init_l, init_m, init_l))
        l0_ref[0:1, :] = l0
        l1_ref[0:1, :] = l1

    l0 = jnp.sum(l0_ref[...], axis=0, keepdims=True)
    l1 = jnp.sum(l1_ref[...], axis=0, keepdims=True)
    lam = (jnp.exp(jnp.sum(lq1_ref[...] * lk1_ref[...], axis=-1, keepdims=True))
           - jnp.exp(jnp.sum(lq2_ref[...] * lk2_ref[...], axis=-1, keepdims=True)) + lam_init)
    out_t = acc0_ref[...] / l0 - lam * (acc1_ref[...] / l1)
    out = _rms(out_t.T) * gsub_ref[...] * (1.0 - lam_init)
    o_ref[...] = out.astype(BF16)


def _attention(z, batch, seq, layer, slopes, lam_params, g_subln3, *, tq, tk):
    t = z.shape[0]
    n_q = seq // tq
    blocks_per_col = D_MODEL // HEAD_WIDTH
    lo, hi, safe, head_stat = _attention_plan(z, batch, seq, slopes, tq=tq, tk=tk)
    pos = jnp.arange(tk, dtype=F32)
    abs_rel = jnp.abs(pos[:, None] - pos[None, :])
    lam_spec = pl.BlockSpec((None, 1, HEAD_DIM), lambda b, h, i, *_: (layer, 0, 0))
    grid_spec = pltpu.PrefetchScalarGridSpec(
        num_scalar_prefetch=3,
        grid=(batch, N_HEADS, n_q),
        in_specs=[
            pl.BlockSpec((tq, HEAD_WIDTH), lambda b, h, i, *_: (b * n_q + i, COL_Q * blocks_per_col + h)),
            pl.BlockSpec((seq, HEAD_WIDTH), lambda b, h, i, *_: (b, COL_K * blocks_per_col + h)),
            pl.BlockSpec((seq, HEAD_WIDTH), lambda b, h, i, *_: (b, COL_VA * blocks_per_col + h)),
            pl.BlockSpec((seq, LANES), lambda b, h, i, *_: (0, 0)),
            pl.BlockSpec((tk, tk), lambda b, h, i, *_: (0, 0)),
            pl.BlockSpec((None, None, 1, LANES), lambda b, h, i, *_: (b, h, 0, 0)),
            lam_spec, lam_spec, lam_spec, lam_spec,
            pl.BlockSpec((None, 1, HEAD_WIDTH), lambda b, h, i, *_: (layer, 0, 0)),
        ],
        out_specs=pl.BlockSpec((tq, HEAD_WIDTH), lambda b, h, i, *_: (b * n_q + i, h)),
        scratch_shapes=[
            pltpu.VMEM((seq // tk, HEAD_WIDTH, tk), BF16),
            pltpu.VMEM((HEAD_WIDTH, tq), F32),
            pltpu.VMEM((HEAD_WIDTH, tq), F32),
            pltpu.VMEM((8, tq), F32),
            pltpu.VMEM((8, tq), F32),
            pltpu.VMEM((2, tk, tq), BF16),
            pltpu.VMEM((tk, tk), F32),
        ],
    )
    return pl.pallas_call(
        functools.partial(_attn_kernel, tk=tk, lam_init=_lambda_init(layer)),
        out_shape=jax.ShapeDtypeStruct((t, D_MODEL), BF16),
        grid_spec=grid_spec,
        compiler_params=_params(3),
        name="diff_attention",
    )(lo, hi, safe, z, z, z, _key_position_features(seq), abs_rel, head_stat, *lam_params, g_subln3)


def _route(scores_t, biased_t):
    def row(a, e):
        return a[e:e + 1, :]

    grp_scores = []
    for g in range(N_GROUPS):
        a, b, c, d = (row(biased_t, g * EXPERTS_PER_GROUP + k) for k in range(EXPERTS_PER_GROUP))
        hi_ab, lo_ab = jnp.maximum(a, b), jnp.minimum(a, b)
        hi_cd, lo_cd = jnp.maximum(c, d), jnp.minimum(c, d)
        top1 = jnp.maximum(hi_ab, hi_cd)
        top2 = jnp.maximum(jnp.minimum(hi_ab, hi_cd), jnp.maximum(lo_ab, lo_cd))
        grp_scores.append(top1 + top2)
    best = grp_scores[0]
    sel = jnp.zeros_like(best, dtype=jnp.int32)
    for g in range(1, N_GROUPS):
        better = grp_scores[g] > best
        best = jnp.where(better, grp_scores[g], best)
        sel = jnp.where(better, g, sel)

    def pick(a, k):
        out = row(a, k)
        for g in range(1, N_GROUPS):
            out = jnp.where(sel == g, row(a, g * EXPERTS_PER_GROUP + k), out)
        return out

    cand_b = [pick(biased_t, k) for k in range(EXPERTS_PER_GROUP)]
    cand_s = [pick(scores_t, k) for k in range(EXPERTS_PER_GROUP)]
    v0, k0, w0 = cand_b[0], jnp.zeros_like(sel), cand_s[0]
    for k in range(1, EXPERTS_PER_GROUP):
        better = cand_b[k] > v0
        v0 = jnp.where(better, cand_b[k], v0)
        k0 = jnp.where(better, k, k0)
        w0 = jnp.where(better, cand_s[k], w0)
    v1 = jnp.full_like(v0, -jnp.inf)
    k1 = jnp.zeros_like(sel)
    w1 = jnp.zeros_like(w0)
    for k in range(EXPERTS_PER_GROUP):
        better = (k0 != k) & (cand_b[k] > v1)
        v1 = jnp.where(better, cand_b[k], v1)
        k1 = jnp.where(better, k, k1)
        w1 = jnp.where(better, cand_s[k], w1)
    total = w0 + w1
    e0 = sel * EXPERTS_PER_GROUP + k0
    e1 = sel * EXPERTS_PER_GROUP + k1
    return sel, e0, e1, w0 / total, w1 / total


PAIRS = [(a, b) for a in range(EXPERTS_PER_GROUP) for b in range(a + 1, EXPERTS_PER_GROUP)]
N_CLASSES = N_GROUPS * len(PAIRS)
CLASS_ROWS = 32


def _combine_weights(h2, wr_ref, br_ref):
    rows = h2.shape[0]
    scores = jax.nn.sigmoid(_dot(h2, wr_ref[...]))
    biased = scores + br_ref[...]
    sel, e0, e1, w0, w1 = _route(scores.T, biased.T)
    expert = lax.broadcasted_iota(jnp.int32, (LANES, rows), 0)
    comb_t = jnp.where(expert == e0, w0, 0.0) + jnp.where(expert == e1, w1, 0.0)
    lo = jnp.minimum(e0, e1) - sel * EXPERTS_PER_GROUP
    gap = jnp.abs(e0 - e1) - 1
    first_pair = sum(jnp.where(lo == a, PAIRS.index((a, a + 1)), 0) for a in range(1, EXPERTS_PER_GROUP - 1))
    return sel * len(PAIRS) + first_pair + gap, comb_t.T


def _to_token_rows(ref, value):
    for s in range(D_MODEL // LANES):
        ref[:, s, :] = value[:, s * LANES:(s + 1) * LANES]


def _from_token_rows(ref):
    return jnp.concatenate([ref[:, s, :] for s in range(D_MODEL // LANES)], axis=1)


def _post_kernel(u_ref, v_ref, ga_ref, gb_ref, o_ref, x_ref, gt_ref, shf_ref, scf_ref, gffn_ref,
                 ws_ref, bs_ref, wa_ref, wb_ref, wo_ref, wr_ref, br_ref, upper_ref,
                 xmid_ref, h2_ref, route_ref, count_ref, a_ref, base_ref):
    tm = x_ref.shape[0]
    for n in range(tm // CHUNK):
        rows = pl.ds(n * CHUNK, CHUNK)
        for g in range(GM_GROUPS):
            cols = pl.ds(g * GROUP_WIDTH, GROUP_WIDTH)
            mixed = _dot(ws_ref[g], v_ref[rows, cols]) + bs_ref[g]
            a_ref[rows, cols] = (u_ref[rows, cols].astype(F32) * mixed).astype(BF16)
    ya = _dot(a_ref[...], wa_ref[...])
    yb = _dot(o_ref[...], wb_ref[...])
    merged = ga_ref[...].astype(F32) * ya + gb_ref[...].astype(F32) * yb
    x_mid = x_ref[...] + gt_ref[...] * _dot(merged.astype(BF16), wo_ref[...])
    xmid_ref[...] = x_mid
    h2 = _rms(x_mid) * gffn_ref[...]
    h2 = (h2 * (1.0 + scf_ref[...]) + shf_ref[...]).astype(BF16)
    _to_token_rows(h2_ref, h2.astype(F32))

    pair_class, _ = _combine_weights(h2, wr_ref, br_ref)

    @pl.when(pl.program_id(0) == 0)
    def _():
        base_ref[...] = jnp.zeros_like(base_ref)

    classes = lax.broadcasted_iota(jnp.int32, (CLASS_ROWS, tm), 0)
    member = classes == pair_class
    earlier = _dot(member.astype(BF16), upper_ref[...])
    base = base_ref[...]
    rank = jnp.sum(jnp.where(member, earlier + base[:, 0:1], 0.0), axis=0, keepdims=True)
    base = base + jnp.sum(member.astype(F32), axis=1, keepdims=True)
    base_ref[...] = base
    count_ref[...] = base
    row = lax.broadcasted_iota(jnp.int32, (8, tm), 0)
    route_ref[...] = jnp.where(row == 0, pair_class.astype(F32), jnp.where(row == 1, rank, 0.0))


def _post_attention(z, o, x, mod3, mod_base, seq, layer, g_ffn3, w_s, bs_b, w_br_a, w_br_b, w_out,
                    w_router_p, b_router_p, *, tm):
    t, d = x.shape
    tiles_per_seq = seq // tm

    def mod_row(which):
        return lambda i: (mod_base + (i // tiles_per_seq) * N_MOD + which, 0, 0)

    def z_col(col):
        return pl.BlockSpec((tm, d), lambda i: (i, col))

    def layer_mat():
        return pl.BlockSpec((None, d, d), lambda i: (layer, 0, 0))

    token = jnp.arange(tm)
    upper = (token[:, None] < token[None, :]).astype(BF16)
    row_tile = pl.BlockSpec((tm, d), lambda i: (i, 0))
    return pl.pallas_call(
        _post_kernel,
        out_shape=(jax.ShapeDtypeStruct((t, d), F32),
                   jax.ShapeDtypeStruct((t, d // LANES, LANES), F32),
                   jax.ShapeDtypeStruct((t // tm, 8, tm), F32),
                   jax.ShapeDtypeStruct((CLASS_ROWS, LANES), F32)),
        grid=(t // tm,),
        in_specs=[
            z_col(COL_U), z_col(COL_V), z_col(COL_GA), z_col(COL_GB),
            row_tile, row_tile,
            pl.BlockSpec((None, 1, d), mod_row(2)),
            pl.BlockSpec((None, 1, d), mod_row(3)),
            pl.BlockSpec((None, 1, d), mod_row(4)),
            pl.BlockSpec((None, 1, d), lambda i: (layer, 0, 0)),
            pl.BlockSpec((None, GM_GROUPS, CHUNK, CHUNK), lambda i: (layer, 0, 0, 0)),
            pl.BlockSpec((None, GM_GROUPS, CHUNK, GROUP_WIDTH), lambda i: (layer, 0, 0, 0)),
            layer_mat(), layer_mat(), layer_mat(),
            pl.BlockSpec((d, LANES), lambda i: (0, 0)),
            pl.BlockSpec((1, LANES), lambda i: (0, 0)),
            pl.BlockSpec((tm, tm), lambda i: (0, 0)),
        ],
        out_specs=(row_tile,
                   pl.BlockSpec((tm, d // LANES, LANES), lambda i: (i, 0, 0)),
                   pl.BlockSpec((None, 8, tm), lambda i: (i, 0, 0)),
                   pl.BlockSpec((CLASS_ROWS, LANES), lambda i: (0, 0))),
        scratch_shapes=[pltpu.VMEM((tm, d), BF16), pltpu.VMEM((CLASS_ROWS, LANES), F32)],
        compiler_params=_params(1),
        name="post_attention",
    )(z, z, z, z, o, x, mod3, mod3, mod3, g_ffn3, w_s, bs_b, w_br_a, w_br_b, w_out,
      w_router_p, b_router_p, upper)


MOE_ROWS = 256
DISPATCH_ROWS = 512
DMA_QUEUES = 2


def _sorted_positions(route, counts, n_tokens):
    token_class = route[:, 0, :].reshape(-1).astype(jnp.int32)
    rank = route[:, 1, :].reshape(-1).astype(jnp.int32)
    counts = counts[:N_CLASSES, 0].astype(jnp.int32)
    padded = (counts + MOE_ROWS - 1) // MOE_ROWS * MOE_ROWS
    ends = jnp.cumsum(padded)
    starts = ends - padded
    n_tiles = n_tokens // MOE_ROWS + N_CLASSES
    position = rank
    for c in range(N_CLASSES):
        position = position + jnp.where(token_class == c, starts[c], 0)
    tile_start = jnp.arange(n_tiles, dtype=jnp.int32) * MOE_ROWS
    tile_class = jnp.zeros((n_tiles,), jnp.int32)
    for c in range(N_CLASSES - 1):
        tile_class = tile_class + (tile_start >= ends[c]).astype(jnp.int32)
    first_expert = tile_class // len(PAIRS) * EXPERTS_PER_GROUP
    pair = tile_class % len(PAIRS)
    expert_a = first_expert + sum(jnp.where(pair == p, a, 0) for p, (a, _) in enumerate(PAIRS))
    expert_b = first_expert + sum(jnp.where(pair == p, b, 0) for p, (_, b) in enumerate(PAIRS))
    rows = min(DISPATCH_ROWS, n_tokens)
    return (expert_a.astype(jnp.int32), expert_b.astype(jnp.int32),
            position.reshape(n_tokens // rows, 1, rows))


def _row_copies(src_hbm, dst, sem, rows):
    return pltpu.make_async_copy(src_hbm.at[pl.ds(0, rows)], dst, sem)


def _dispatch_kernel(pos_ref, h_ref, init_hbm, x_hbm, sem):
    del init_hbm
    rows = pos_ref.shape[1]

    def body(pair, carry):
        for lane_of_pair in range(DMA_QUEUES):
            r = pair * DMA_QUEUES + lane_of_pair
            pltpu.make_async_copy(h_ref.at[r], x_hbm.at[pos_ref[0, r]], sem).start(priority=lane_of_pair)
        return carry
    lax.fori_loop(0, rows // DMA_QUEUES, body, 0)
    pltpu.make_async_copy(h_ref, x_hbm.at[pl.ds(0, rows)], sem).wait()


def _dispatch(h2_rows, position, n_rows):
    n_tokens, sub, _ = h2_rows.shape
    n_steps, _, rows = position.shape
    return pl.pallas_call(
        _dispatch_kernel,
        out_shape=jax.ShapeDtypeStruct((n_rows, sub, LANES), F32),
        grid=(n_steps,),
        in_specs=[
            pl.BlockSpec((None, 1, rows), lambda i: (i, 0, 0), memory_space=pltpu.SMEM),
            pl.BlockSpec((rows, sub, LANES), lambda i: (i, 0, 0)),
            pl.BlockSpec(memory_space=pl.ANY),
        ],
        out_specs=pl.BlockSpec(memory_space=pl.ANY),
        scratch_shapes=[pltpu.SemaphoreType.DMA(())],
        input_output_aliases={2: 0},
        compiler_params=_params(1),
        name="moe_dispatch",
    )(position, h2_rows, jnp.zeros((n_rows, sub, LANES), F32))


def _moe_kernel(expert_a_ref, expert_b_ref, x_ref, wr_ref, br_ref,
                wga_ref, wua_ref, wda_ref, wgb_ref, wub_ref, wdb_ref, y_ref):
    g = pl.program_id(0)
    h = _from_token_rows(x_ref).astype(BF16)
    _, comb = _combine_weights(h, wr_ref, br_ref)
    lane = lax.broadcasted_iota(jnp.int32, comb.shape, 1)
    y = jnp.zeros((h.shape[0], D_MODEL), F32)
    for expert, wg_ref, wu_ref, wd_ref in ((expert_a_ref[g], wga_ref, wua_ref, wda_ref),
                                           (expert_b_ref[g], wgb_ref, wub_ref, wdb_ref)):
        gate = _dot(h, wg_ref[...])
        hid = gate * jax.nn.sigmoid(gate) * _dot(h, wu_ref[...])
        weight = jnp.sum(jnp.where(lane == expert, comb, 0.0), axis=-1, keepdims=True)
        y = y + _dot((weight * hid).astype(BF16), wd_ref[...])
    _to_token_rows(y_ref, y)


def _moe(x_sorted, expert_a, expert_b, layer, w_router_p, b_router_p, w_gate, w_up, w_down):
    n_rows, sub, _ = x_sorted.shape
    d = sub * LANES
    row_tile = pl.BlockSpec((MOE_ROWS, sub, LANES), lambda g, ea, eb: (g, 0, 0))

    def weights_of(which, a, b):
        return pl.BlockSpec((None, None, a, b), lambda g, ea, eb: (layer, (ea, eb)[which][g], 0, 0))

    grid_spec = pltpu.PrefetchScalarGridSpec(
        num_scalar_prefetch=2,
        grid=(n_rows // MOE_ROWS,),
        in_specs=[
            row_tile,
            pl.BlockSpec((d, LANES), lambda g, ea, eb: (0, 0)),
            pl.BlockSpec((1, LANES), lambda g, ea, eb: (0, 0)),
            weights_of(0, d, D_EXPERT), weights_of(0, d, D_EXPERT), weights_of(0, D_EXPERT, d),
            weights_of(1, d, D_EXPERT), weights_of(1, d, D_EXPERT), weights_of(1, D_EXPERT, d),
        ],
        out_specs=row_tile,
    )
    return pl.pallas_call(
        _moe_kernel,
        out_shape=jax.ShapeDtypeStruct((n_rows, sub, LANES), F32),
        grid_spec=grid_spec,
        compiler_params=_params(1),
        name="moe_experts",
    )(expert_a, expert_b, x_sorted, w_router_p, b_router_p, w_gate, w_up, w_down, w_gate, w_up, w_down)


def _residual_kernel(pos_ref, pos_next_ref, x_ref, gt_ref, gfin_ref, y_hbm, out_ref, y_buf, sem,
                     *, final_norm):
    i = pl.program_id(0)
    slot = i % 2
    rows = y_buf.shape[1]

    def start_gather(index_ref, to_slot):
        def body(pair, carry):
            for lane_of_pair in range(DMA_QUEUES):
                r = pair * DMA_QUEUES + lane_of_pair
                pltpu.make_async_copy(y_hbm.at[index_ref[0, r]], y_buf.at[to_slot, r],
                                      sem.at[to_slot]).start(priority=lane_of_pair)
            return carry
        lax.fori_loop(0, rows // DMA_QUEUES, body, 0)

    @pl.when(i == 0)
    def _():
        start_gather(pos_ref, 0)

    @pl.when(i + 1 < pl.num_programs(0))
    def _():
        start_gather(pos_next_ref, 1 - slot)

    _row_copies(y_hbm, y_buf.at[slot], sem.at[slot], rows).wait()
    x = x_ref[...] + gt_ref[...] * _from_token_rows(y_buf.at[slot])
    if final_norm:
        x = _rms(x) * gfin_ref[...]
    out_ref[...] = x


def _moe_residual(x_mid, y_sorted, position, mod3, mod_base, seq, g_final2, *, final_norm):
    t, d = x_mid.shape
    n_steps, _, rows = position.shape
    tiles_per_seq = seq // rows
    row_tile = pl.BlockSpec((rows, d), lambda i: (i, 0))
    index_block = lambda offset: pl.BlockSpec(
        (None, 1, rows), lambda i: (jnp.minimum(i + offset, n_steps - 1), 0, 0), memory_space=pltpu.SMEM)
    return pl.pallas_call(
        functools.partial(_residual_kernel, final_norm=final_norm),
        out_shape=jax.ShapeDtypeStruct((t, d), F32),
        grid=(n_steps,),
        in_specs=[
            index_block(0), index_block(1),
            row_tile,
            pl.BlockSpec((None, 1, d), lambda i: (mod_base + (i // tiles_per_seq) * N_MOD + 5, 0, 0)),
            pl.BlockSpec((1, d), lambda i: (0, 0)),
            pl.BlockSpec(memory_space=pl.ANY),
        ],
        out_specs=row_tile,
        scratch_shapes=[pltpu.VMEM((2, rows, d // LANES, LANES), F32), pltpu.SemaphoreType.DMA((2,))],
        compiler_params=_params(1),
        name="moe_residual",
    )(position, position, x_mid, mod3, g_final2, y_sorted)


def _tile(seq, target):
    return min(seq, target)


def _trunk(x3, mod3, batch_offset, padded_batch, w, depth):
    batch, seq, d = x3.shape
    x = x3.reshape(batch * seq, d)
    for layer in range(depth):
        mod_base = (layer * padded_batch + batch_offset) * N_MOD
        z = _in_projection(x, mod3, mod_base, seq, layer, w["g_mix"], w["w_in"], w["g_sg"],
                           tm=_tile(seq, 1024))
        o = _attention(z, batch, seq, layer, w["slopes"], w["lam"], w["g_subln"],
                       tq=_tile(seq, 1024), tk=_tile(seq, 512))
        x_mid, h2_rows, route, counts = _post_attention(
            z, o, x, mod3, mod_base, seq, layer, w["g_ffn"], w["w_s"], w["bs_b"], w["w_br_a"],
            w["w_br_b"], w["w_out"], w["w_router"], w["b_router"], tm=_tile(seq, 512))
        expert_a, expert_b, position = _sorted_positions(route, counts, batch * seq)
        x_sorted = _dispatch(h2_rows, position, expert_a.shape[0] * MOE_ROWS)
        y_sorted = _moe(x_sorted, expert_a, expert_b, layer, w["w_router"], w["b_router"],
                        w["w_gate"], w["w_up"], w["w_down"])
        x = _moe_residual(x_mid, y_sorted, position, mod3, mod_base, seq, w["g_final"],
                          final_norm=(layer == depth - 1))
    return x.reshape(batch, seq, d)


def kernel(x_prompt, x_sample, c_prompt, c_sample, w_ada, b_ada, g_mix, w_in, g_sg, w_s, b_s, w_br_a,
           lam_q1, lam_k1, lam_q2, lam_k2, g_subln, w_br_b, w_out, g_ffn, w_router, b_router,
           w_gate, w_up, w_down, g_final):
    depth, d = g_mix.shape
    n_prompt, n_sample = c_prompt.shape[0], c_sample.shape[0]
    padded_batch = -(-(n_prompt + n_sample) // 8) * 8
    c_all = jnp.zeros((padded_batch, d), F32)
    c_all = c_all.at[:n_prompt].set(c_prompt).at[n_prompt:n_prompt + n_sample].set(c_sample)
    mod = _modulations(c_all, w_ada, b_ada)
    mod3 = mod.reshape(depth * padded_batch * N_MOD, 1, d)

    slopes = jnp.exp2(-(8.0 / N_HEADS) * jnp.arange(1, N_HEADS + 1, dtype=F32))
    w = {
        "g_mix": g_mix.reshape(depth, 1, d),
        "w_in": w_in.astype(BF16),
        "g_sg": g_sg.reshape(depth, 1, d),
        "slopes": slopes,
        "lam": tuple(p.reshape(depth, 1, HEAD_DIM) for p in (lam_q1, lam_k1, lam_q2, lam_k2)),
        "g_subln": g_subln.reshape(depth, 1, HEAD_WIDTH),
        "g_ffn": g_ffn.reshape(depth, 1, d),
        "w_s": w_s.astype(BF16),
        "bs_b": jnp.broadcast_to(b_s[..., None], b_s.shape + (GROUP_WIDTH,)),
        "w_br_a": w_br_a.astype(BF16),
        "w_br_b": w_br_b.astype(BF16),
        "w_out": w_out.astype(BF16),
        "w_router": jnp.pad(w_router, ((0, 0), (0, LANES - N_EXPERTS))).astype(BF16),
        "b_router": jnp.pad(b_router, (0, LANES - N_EXPERTS)).reshape(1, LANES),
        "w_gate": w_gate.astype(BF16),
        "w_up": w_up.astype(BF16),
        "w_down": w_down.astype(BF16),
        "g_final": g_final.reshape(1, d),
    }
    y_prompt = _trunk(x_prompt, mod3, 0, padded_batch, w, depth)
    y_sample = _trunk(x_sample, mod3, n_prompt, padded_batch, w, depth)
    return (y_prompt, y_sample)
```
